```python
import math
import jax
import jax.numpy as jnp
from jax import lax
import numpy as np

D_MODEL = 1024
BATCH = 8
SEQ = 4096
DEPTH = 4

A_WIDTH = 256
A_GROUPS = 4
A_GDIM = A_WIDTH // A_GROUPS
CHUNK = 128
B_WIDTH = 256
POOL_WINDOWS = (2, 4, 8, 16)
B_GROUPS = len(POOL_WINDOWS)
B_GDIM = B_WIDTH // B_GROUPS
C_HEADS = 4
C_HEAD_DIM = 64
C_VDIM = 2 * C_HEAD_DIM
C_QK_WIDTH = C_HEADS * 2 * C_HEAD_DIM
C_V_WIDTH = C_HEADS * C_VDIM
Q_BLOCK = 128
N_BUCKETS = 32
MAX_EXACT = 16
MAX_DISTANCE = 128
IN_COLS = 2 * A_WIDTH + B_WIDTH + 2 * C_QK_WIDTH + C_V_WIDTH
N_BRANCH = 3
D_FF = 2816
CONV_WIDTH = 3
EPS = 1e-6

kernel_name = 'hybrid_gated_parallel_block'


def rmsnorm(x, g, eps=EPS):
    xf = x.astype(jnp.float32)
    y = xf * lax.rsqrt(jnp.mean(xf * xf, axis=-1, keepdims=True) + eps)
    return (y * g.astype(jnp.float32)).astype(x.dtype)


def layernorm(x, g, b, eps=1e-5):
    xf = x.astype(jnp.float32)
    mu = jnp.mean(xf, axis=-1, keepdims=True)
    var = jnp.mean(jnp.square(xf - mu), axis=-1, keepdims=True)
    y = (xf - mu) * lax.rsqrt(var + eps)
    return (y * g.astype(jnp.float32) + b.astype(jnp.float32)).astype(x.dtype)


def gmlp_mixer(zuv, ln_g, ln_b, w_s, b_s):
    B, S, _ = zuv.shape
    z = jax.nn.gelu(zuv, approximate=False)
    u, v = z[..., :A_WIDTH], z[..., A_WIDTH:]
    v = layernorm(v, ln_g, ln_b)
    vc = v.reshape(B, S // CHUNK, CHUNK, A_GROUPS, A_GDIM)
    mask = jnp.tril(jnp.ones((CHUNK, CHUNK), dtype=bool))
    ws = jnp.where(mask[None], w_s, jnp.zeros_like(w_s))
    mixed = jnp.einsum('gts,bcsgd->bctgd', ws, vc) + b_s.T[None, None, :, :, None]
    return u * mixed.reshape(B, S, A_WIDTH)


def pool_mixer(xb, pool_w, pool_scale):
    B, S, _ = xb.shape
    xf = xb.astype(jnp.float32)
    cs = jnp.cumsum(xf, axis=1)
    t = jnp.arange(S)
    outs = []
    for gi, w in enumerate(POOL_WINDOWS):
        sl = slice(gi * B_GDIM, (gi + 1) * B_GDIM)
        c = cs[..., sl]
        lag = jnp.pad(c, ((0, 0), (w, 0), (0, 0)))[:, :S]
        cnt = jnp.minimum(t + 1, w).astype(jnp.float32)[None, :, None]
        outs.append((c - lag) / cnt - xf[..., sl])
    p = jnp.stack(outs, axis=2).astype(xb.dtype)
    y = jnp.einsum('bsgc,gcd->bsgd', p, pool_w).reshape(B, S, B_WIDTH)
    return y * pool_scale


def t5_causal_bucket(rel):
    n = jnp.maximum(rel, 0)
    nf = jnp.maximum(n, 1).astype(jnp.float32)
    large = MAX_EXACT + (jnp.log(nf / MAX_EXACT) / math.log(MAX_DISTANCE / MAX_EXACT)
                         * (N_BUCKETS - MAX_EXACT)).astype(jnp.int32)
    large = jnp.minimum(large, N_BUCKETS - 1)
    return jnp.where(n < MAX_EXACT, n, large)


def diff_attention(q, k, v, lam, rel_bias):
    B, S, H, _, dh = q.shape
    nb = S // Q_BLOCK
    scale = dh ** -0.5
    qb = q.reshape(B, nb, Q_BLOCK, H, 2, dh).transpose(1, 0, 2, 3, 4, 5)
    kpos = jnp.arange(S)

    def block(args):
        qi, i = args
        qpos = i * Q_BLOCK + jnp.arange(Q_BLOCK)
        rel = qpos[:, None] - kpos[None, :]
        bias = rel_bias[t5_causal_bucket(rel)].astype(jnp.float32)
        s = jnp.einsum('bqhcd,bkhcd->bhcqk', qi, k).astype(jnp.float32) * scale
        s = s + bias.transpose(2, 0, 1)[None, :, None]
        s = jnp.where((rel >= 0)[None, None, None], s, -1e30)
        p = jax.nn.softmax(s, axis=-1)
        a = p[:, :, 0] - lam * p[:, :, 1]
        return jnp.einsum('bhqk,bkhd->bqhd', a.astype(v.dtype), v)

    out = lax.map(block, (qb, jnp.arange(nb)))
    return out.transpose(1, 0, 2, 3, 4).reshape(B, S, H, v.shape[-1])


def diff_attn_mixer(zq, zk, zv, lam_vecs, subln_g, rel_bias, layer_idx):
    B, S, _ = zq.shape
    q = zq.reshape(B, S, C_HEADS, 2, C_HEAD_DIM)
    k = zk.reshape(B, S, C_HEADS, 2, C_HEAD_DIM)
    v = zv.reshape(B, S, C_HEADS, C_VDIM)
    lam_init = 0.8 - 0.6 * math.exp(-0.3 * layer_idx)
    lv = lam_vecs.astype(jnp.float32)
    lam = jnp.exp(jnp.sum(lv[0] * lv[1])) - jnp.exp(jnp.sum(lv[2] * lv[3])) + lam_init
    o = diff_attention(q, k, v, lam, rel_bias)
    o = rmsnorm(o, subln_g, eps=1e-5) * (1.0 - lam_init)
    return o.reshape(B, S, C_V_WIDTH)


def conv_ffn(h, w_up, conv_w, conv_b, w_down):
    gu = h @ w_up
    gate, up = gu[..., :D_FF], gu[..., D_FF:]
    gate = lax.conv_general_dilated(
        gate, conv_w[:, None, :], window_strides=(1,), padding=[(CONV_WIDTH - 1, 0)],
        dimension_numbers=('NWC', 'WIO', 'NWC'), feature_group_count=D_FF) + conv_b
    return (jax.nn.gelu(gate, approximate=False) * up) @ w_down


def setup_inputs(seed: int = 0) -> dict:
    key = jax.random.key(seed)
    ks = jax.random.split(key, 24)
    L, D = DEPTH, D_MODEL

    def nrm(k, shape, scale):
        return jax.random.normal(k, shape, jnp.float32) * scale

    def gain(k, shape):
        return 1.0 + nrm(k, shape, 0.05)

    return {
        'x': nrm(ks[0], (BATCH, SEQ, D), 1.0),
        'attn_norm_g': gain(ks[1], (L, D)),
        'w_in': nrm(ks[2], (L, D, IN_COLS), D ** -0.5),
        'w_gate': nrm(ks[3], (L, D, N_BRANCH * D), D ** -0.5),
        'b_gate': nrm(ks[4], (L, N_BRANCH * D), 0.01),
        'sgu_ln_g': gain(ks[5], (L, A_WIDTH)),
        'sgu_ln_b': nrm(ks[6], (L, A_WIDTH), 0.01),
        'sgu_w': nrm(ks[7], (L, A_GROUPS, CHUNK, CHUNK), CHUNK ** -0.5),
        'sgu_b': gain(ks[8], (L, A_GROUPS, CHUNK)),
        'proj_a': nrm(ks[9], (L, A_WIDTH, D), A_WIDTH ** -0.5),
        'pool_w': nrm(ks[10], (L, B_GROUPS, B_GDIM, B_GDIM), B_GDIM ** -0.5),
        'pool_scale': gain(ks[11], (L, B_WIDTH)),
        'proj_b': nrm(ks[12], (L, B_WIDTH, D), B_WIDTH ** -0.5),
        'diff_lam': nrm(ks[13], (L, 4, C_HEAD_DIM), 0.1),
        'diff_subln_g': gain(ks[14], (L, C_VDIM)),
        'proj_c': nrm(ks[15], (L, C_V_WIDTH, D), C_V_WIDTH ** -0.5),
        'w_out': nrm(ks[16], (L, D, D), D ** -0.5),
        'ffn_norm_g': gain(ks[17], (L, D)),
        'w_up': nrm(ks[18], (L, D, 2 * D_FF), D ** -0.5),
        'conv_w': nrm(ks[19], (L, CONV_WIDTH, D_FF), CONV_WIDTH ** -0.5),
        'conv_b': nrm(ks[20], (L, D_FF), 0.01),
        'w_down': nrm(ks[21], (L, D_FF, D), D_FF ** -0.5),
        'rel_bias': nrm(ks[22], (N_BUCKETS, C_HEADS), 0.5),
        'final_norm_g': gain(ks[23], (D,)),
    }


def reference(x, attn_norm_g, w_in, w_gate, b_gate, sgu_ln_g, sgu_ln_b, sgu_w, sgu_b,
              proj_a, pool_w, pool_scale, proj_b, diff_lam, diff_subln_g, proj_c, w_out,
              ffn_norm_g, w_up, conv_w, conv_b, w_down, rel_bias, final_norm_g):
    o_a = 2 * A_WIDTH
    o_b = o_a + B_WIDTH
    o_q = o_b + C_QK_WIDTH
    o_k = o_q + C_QK_WIDTH
    for l in range(DEPTH):
        h = rmsnorm(x, attn_norm_g[l])
        z = h @ w_in[l]
        gates = jax.nn.sigmoid(h @ w_gate[l] + b_gate[l])
        y_a = gmlp_mixer(z[..., :o_a], sgu_ln_g[l], sgu_ln_b[l], sgu_w[l], sgu_b[l])
        y_b = pool_mixer(z[..., o_a:o_b], pool_w[l], pool_scale[l])
        y_c = diff_attn_mixer(z[..., o_b:o_q], z[..., o_q:o_k], z[..., o_k:],
                              diff_lam[l], diff_subln_g[l], rel_bias, l)
        g_a = gates[..., :D_MODEL]
        g_b = gates[..., D_MODEL:2 * D_MODEL]
        g_c = gates[..., 2 * D_MODEL:]
        merged = g_a * (y_a @ proj_a[l]) + g_b * (y_b @ proj_b[l]) + g_c * (y_c @ proj_c[l])
        x = x + merged @ w_out[l]
        x = x + conv_ffn(rmsnorm(x, ffn_norm_g[l]), w_up[l], conv_w[l], conv_b[l], w_down[l])
    return rmsnorm(x, final_norm_g)
```

```python
import functools
import math

import numpy as np
import jax
import jax.numpy as jnp
from jax import lax
from jax.experimental import pallas as pl
from jax.experimental.pallas import tpu as pltpu

D_MODEL = 1024
DEPTH = 4
A_WIDTH = 256
A_GROUPS = 4
A_GDIM = A_WIDTH // A_GROUPS
CHUNK = 128
B_WIDTH = 256
POOL_WINDOWS = (2, 4, 8, 16)
B_GDIM = B_WIDTH // len(POOL_WINDOWS)
C_HEADS = 4
C_HEAD_DIM = 64
C_VDIM = 2 * C_HEAD_DIM
C_QK_WIDTH = C_HEADS * 2 * C_HEAD_DIM
C_V_WIDTH = C_HEADS * C_VDIM
N_BUCKETS = 32
MAX_EXACT = 16
MAX_DISTANCE = 128
D_FF = 2816
EPS = 1e-6
SUBLN_EPS = 1e-5
LN_EPS = 1e-5
MASK_VALUE = -1e30

O_A = 2 * A_WIDTH
O_B = O_A + B_WIDTH
O_Q = O_B + C_QK_WIDTH
O_K = O_Q + C_QK_WIDTH
IN_COLS = O_K + C_V_WIDTH

V7X_VMEM_LIMIT_BYTES = 56 * 1024 * 1024
HALO = 16
TM_IN = 512
TM_MERGE = 512
TM_FFN = 512
FF_CHUNK = 256
TQ = 256
TK = 256

F32 = jnp.float32
BF16 = jnp.bfloat16


def _dot(a, b):
    return jnp.dot(a, b, preferred_element_type=F32)


def _gelu(x):
    return 0.5 * x * (1.0 + lax.erf(x * np.float32(math.sqrt(0.5))))


def _rmsnorm(x, g, eps):
    return x * lax.rsqrt(jnp.mean(x * x, axis=-1, keepdims=True) + eps) * g


def _const_spec(shape):
    nd = len(shape)
    return pl.BlockSpec(shape, lambda *_: (0,) * nd, pipeline_mode=pl.Buffered(1))


def _in_proj_kernel(x_ref, g_ref, w_ref, lng_ref, lnb_ref, sw_ref, sb_ref, pw_ref, ps_ref,
                    ya_ref, yb_ref, q_ref, k_ref, v_ref, zb_scr):
    t = pl.program_id(1)
    tm = x_ref.shape[1]
    h = _rmsnorm(x_ref[0], g_ref[...], EPS).astype(BF16)

    za = _gelu(_dot(h, w_ref[:, 0:O_A]))
    u = za[:, :A_WIDTH]
    v = za[:, A_WIDTH:]
    mu = jnp.mean(v, axis=-1, keepdims=True)
    vc = v - mu
    var = jnp.mean(vc * vc, axis=-1, keepdims=True)
    vn = (vc * lax.rsqrt(var + LN_EPS) * lng_ref[...] + lnb_ref[...]).astype(BF16)
    row = lax.broadcasted_iota(jnp.int32, (CHUNK, A_GROUPS * CHUNK), 0)
    col = lax.broadcasted_iota(jnp.int32, (CHUNK, A_GROUPS * CHUNK), 1)
    ws = jnp.where((col % CHUNK) <= row, sw_ref[...], 0.0).astype(BF16)
    grp = lax.broadcasted_iota(jnp.int32, (CHUNK, A_WIDTH), 1) // A_GDIM
    for c in range(tm // CHUNK):
        vch = vn[c * CHUNK:(c + 1) * CHUNK, :]
        rhs = jnp.concatenate(
            [jnp.where(grp == g, vch, jnp.zeros_like(vch)) for g in range(A_GROUPS)], axis=0)
        mixed = _dot(ws, rhs) + sb_ref[...]
        ya_ref[0, c * CHUNK:(c + 1) * CHUNK, :] = (
            u[c * CHUNK:(c + 1) * CHUNK, :] * mixed).astype(ya_ref.dtype)

    zb = _dot(h, w_ref[:, O_A:O_B])

    @pl.when(t == 0)
    def _():
        zb_scr[0:HALO, :] = jnp.zeros((HALO, B_WIDTH), F32)

    @pl.when(t > 0)
    def _():
        zb_scr[0:HALO, :] = zb_scr[tm:tm + HALO, :]

    zb_scr[HALO:HALO + tm, :] = zb
    pos = t * tm + lax.broadcasted_iota(jnp.int32, (tm, 1), 0)
    lane = lax.broadcasted_iota(jnp.int32, (tm, 128), 1)
    halves = []
    for half in range(2):
        cols = slice(half * 128, (half + 1) * 128)
        w_small, w_big = POOL_WINDOWS[2 * half], POOL_WINDOWS[2 * half + 1]
        acc = zb[:, cols]
        for k in range(1, w_small):
            acc = acc + zb_scr[HALO - k:HALO - k + tm, cols]
        s_small = acc
        for k in range(w_small, w_big):
            acc = acc + zb_scr[HALO - k:HALO - k + tm, cols]
        cnt_small = jnp.minimum(pos + 1, w_small).astype(F32)
        cnt_big = jnp.minimum(pos + 1, w_big).astype(F32)
        pooled = jnp.where(lane < B_GDIM, s_small / cnt_small, acc / cnt_big)
        halves.append(pooled - zb[:, cols])
    p = jnp.concatenate(halves, axis=1).astype(BF16)
    yb_ref[0] = (_dot(p, pw_ref[...]) * ps_ref[...]).astype(yb_ref.dtype)

    zq = _dot(h, w_ref[:, O_B:O_Q]) * np.float32(C_HEAD_DIM ** -0.5)
    first_half = (lax.broadcasted_iota(jnp.int32, (tm, C_QK_WIDTH), 1) % C_VDIM) < C_HEAD_DIM
    q_ref[0, 0] = jnp.where(first_half, zq, 0.0).astype(q_ref.dtype)
    q_ref[0, 1] = jnp.where(first_half, 0.0, zq).astype(q_ref.dtype)
    k_ref[0] = _dot(h, w_ref[:, O_Q:O_K]).astype(k_ref.dtype)
    v_ref[0] = _dot(h, w_ref[:, O_K:IN_COLS]).astype(v_ref.dtype)


def _in_proj(x, g, w_in, ln_g, ln_b, sgu_w_cat, sgu_b_tile, pool_w_bd, pool_scale):
    B, S, D = x.shape
    tm = TM_IN
    grid = (B, S // tm)
    tok = lambda width: pl.BlockSpec((1, tm, width), lambda b, t: (b, t, 0))
    out_shape = (
        jax.ShapeDtypeStruct((B, S, A_WIDTH), BF16),
        jax.ShapeDtypeStruct((B, S, B_WIDTH), BF16),
        jax.ShapeDtypeStruct((B, 2, S, C_QK_WIDTH), BF16),
        jax.ShapeDtypeStruct((B, S, C_QK_WIDTH), BF16),
        jax.ShapeDtypeStruct((B, S, C_V_WIDTH), BF16),
    )
    return pl.pallas_call(
        _in_proj_kernel,
        grid=grid,
        in_specs=[
            tok(D),
            _const_spec((1, D)),
            _const_spec((D, IN_COLS)),
            _const_spec((1, A_WIDTH)),
            _const_spec((1, A_WIDTH)),
            _const_spec((CHUNK, A_GROUPS * CHUNK)),
            _const_spec((CHUNK, A_WIDTH)),
            _const_spec((B_WIDTH, B_WIDTH)),
            _const_spec((1, B_WIDTH)),
        ],
        out_specs=(
            tok(A_WIDTH),
            tok(B_WIDTH),
            pl.BlockSpec((1, 2, tm, C_QK_WIDTH), lambda b, t: (b, 0, t, 0)),
            tok(C_QK_WIDTH),
            tok(C_V_WIDTH),
        ),
        out_shape=out_shape,
        scratch_shapes=[pltpu.VMEM((HALO + tm, B_WIDTH), F32)],
        compiler_params=pltpu.CompilerParams(
            dimension_semantics=("arbitrary", "arbitrary"),
            vmem_limit_bytes=V7X_VMEM_LIMIT_BYTES),
        name="in_proj",
    )(x, g, w_in, ln_g, ln_b, sgu_w_cat, sgu_b_tile, pool_w_bd, pool_scale)


def _bucket_tiles():
    r = np.arange(TQ)[:, None]
    c = np.arange(TK)[None, :]
    tiles = []
    for d in range(2):
        rel = d * TQ + r - c
        n = np.maximum(rel, 0)
        nf = np.maximum(n, 1).astype(np.float32)
        large = MAX_EXACT + (np.log(nf / MAX_EXACT) / math.log(MAX_DISTANCE / MAX_EXACT)
                             * (N_BUCKETS - MAX_EXACT)).astype(np.int32)
        large = np.minimum(large, N_BUCKETS - 1)
        bucket = np.where(n < MAX_EXACT, n, large)
        tiles.append(np.where(rel >= 0, bucket, -1))
    return np.stack(tiles).astype(np.int32)


def _bias_kernel(rb_ref, bk_ref, o_ref):
    for h in range(C_HEADS):
        far = rb_ref[N_BUCKETS - 1, h]
        for d in range(2):
            bk = bk_ref[d]
            acc = jnp.zeros(bk.shape, F32)
            for b in range(N_BUCKETS - 1):
                acc = jnp.where(bk == b, rb_ref[b, h] - far, acc)
            o_ref[h, d] = jnp.where(bk < 0, np.float32(MASK_VALUE), acc)


def _bias_tiles(rel_bias):
    buckets = jnp.asarray(_bucket_tiles())
    return pl.pallas_call(
        _bias_kernel,
        in_specs=[pl.BlockSpec(memory_space=pltpu.SMEM),
                  pl.BlockSpec(memory_space=pltpu.VMEM)],
        out_specs=pl.BlockSpec(memory_space=pltpu.VMEM),
        out_shape=jax.ShapeDtypeStruct((C_HEADS, 2, TQ, TK), F32),
        name="bias_tiles",
    )(rel_bias, buckets)


def _attn_kernel(sc_ref, q_ref, k_ref, v_ref, bias_ref, lamv_ref, g_ref, o_ref,
                 m_scr, l_scr, acc_scr):
    qi = pl.program_id(1)
    lam_init = sc_ref[0]
    lv = lamv_ref[...]
    lam = (jnp.exp(jnp.sum(lv[0:1] * lv[1:2], axis=-1, keepdims=True))
           - jnp.exp(jnp.sum(lv[2:3] * lv[3:4], axis=-1, keepdims=True)) + lam_init)

    for h in range(C_HEADS):
        cols = slice(h * C_VDIM, (h + 1) * C_VDIM)
        qs = jnp.concatenate([q_ref[0, 0, :, cols], q_ref[0, 1, :, cols]], axis=0)
        m_scr[...] = jnp.full(m_scr.shape, -jnp.inf, F32)
        l_scr[...] = jnp.zeros(l_scr.shape, F32)
        acc_scr[...] = jnp.zeros(acc_scr.shape, F32)

        def step(j, bias):
            start = pl.multiple_of(j * TK, TK)
            kt = k_ref[0, pl.ds(start, TK), cols]
            vt = v_ref[0, pl.ds(start, TK), cols]
            s = lax.dot_general(qs, kt, (((1,), (1,)), ((), ())), preferred_element_type=F32)
            if bias is not None:
                s = s + jnp.concatenate([bias, bias], axis=0)
            m_prev = m_scr[...]
            m_new = jnp.maximum(m_prev, jnp.max(s, axis=-1, keepdims=True))
            alpha = jnp.exp(m_prev - m_new)
            p = jnp.exp(s - m_new)
            l_scr[...] = alpha * l_scr[...] + jnp.sum(p, axis=-1, keepdims=True)
            acc_scr[...] = alpha * acc_scr[...] + _dot(p.astype(BF16), vt)
            m_scr[...] = m_new

        def plain_step(j, carry):
            step(j, None)
            return carry

        lax.fori_loop(0, jnp.maximum(qi - 1, 0), plain_step, 0)

        @pl.when(qi >= 1)
        def _():
            step(qi - 1, bias_ref[h, 1])

        step(qi, bias_ref[h, 0])

        o_all = acc_scr[...] / l_scr[...]
        o = o_all[:TQ] - lam * o_all[TQ:]
        o = _rmsnorm(o, g_ref[...], SUBLN_EPS) * (1.0 - lam_init)
        o_ref[0, :, cols] = o.astype(o_ref.dtype)


def _attention(scalars, q, k, v, bias, lam_vecs, subln_g):
    B, S, _ = k.shape
    grid = (B, S // TQ)
    return pl.pallas_call(
        _attn_kernel,
        grid=grid,
        in_specs=[
            pl.BlockSpec(memory_space=pltpu.SMEM),
            pl.BlockSpec((1, 2, TQ, C_QK_WIDTH), lambda b, i: (b, 0, i, 0)),
            pl.BlockSpec((1, S, C_QK_WIDTH), lambda b, i: (b, 0, 0)),
            pl.BlockSpec((1, S, C_V_WIDTH), lambda b, i: (b, 0, 0)),
            _const_spec((C_HEADS, 2, TQ, TK)),
            _const_spec((4, C_HEAD_DIM)),
            _const_spec((1, C_VDIM)),
        ],
        out_specs=pl.BlockSpec((1, TQ, C_V_WIDTH), lambda b, i: (b, i, 0)),
        out_shape=jax.ShapeDtypeStruct((B, S, C_V_WIDTH), BF16),
        scratch_shapes=[
            pltpu.VMEM((2 * TQ, 1), F32),
            pltpu.VMEM((2 * TQ, 1), F32),
            pltpu.VMEM((2 * TQ, C_VDIM), F32),
        ],
        compiler_params=pltpu.CompilerParams(
            dimension_semantics=("arbitrary", "arbitrary"),
            vmem_limit_bytes=V7X_VMEM_LIMIT_BYTES),
        name="diff_attn",
    )(scalars, q, k, v, bias, lam_vecs, subln_g)


def _merge_kernel(x_ref, ya_ref, yb_ref, yc_ref, g_ref, wg_ref, bg_ref,
                  pa_ref, pb_ref, pc_ref, wo_ref, o_ref):
    x = x_ref[...]
    h = _rmsnorm(x, g_ref[...], EPS).astype(BF16)
    merged = None
    for i, (y_ref, p_ref) in enumerate(((ya_ref, pa_ref), (yb_ref, pb_ref), (yc_ref, pc_ref))):
        cols = slice(i * D_MODEL, (i + 1) * D_MODEL)
        pre = _dot(h, wg_ref[:, cols]) + bg_ref[:, cols]
        gate = 1.0 / (1.0 + jnp.exp(-pre))
        term = gate * _dot(y_ref[...], p_ref[...])
        merged = term if merged is None else merged + term
    o_ref[...] = x + _dot(merged.astype(BF16), wo_ref[...])


def _merge(x2, ya, yb, yc, g, w_gate, b_gate, proj_a, proj_b, proj_c, w_out):
    N, D = x2.shape
    tm = TM_MERGE
    tok = lambda width: pl.BlockSpec((tm, width), lambda t: (t, 0))
    return pl.pallas_call(
        _merge_kernel,
        grid=(N // tm,),
        in_specs=[
            tok(D), tok(A_WIDTH), tok(B_WIDTH), tok(C_V_WIDTH),
            _const_spec((1, D)),
            _const_spec((D, 3 * D)),
            _const_spec((1, 3 * D)),
            _const_spec((A_WIDTH, D)),
            _const_spec((B_WIDTH, D)),
            _const_spec((C_V_WIDTH, D)),
            _const_spec((D, D)),
        ],
        out_specs=tok(D),
        out_shape=jax.ShapeDtypeStruct((N, D), F32),
        compiler_params=pltpu.CompilerParams(
            dimension_semantics=("arbitrary",),
            vmem_limit_bytes=V7X_VMEM_LIMIT_BYTES),
        name="merge",
    )(x2, ya, yb, yc, g, w_gate, b_gate, proj_a, proj_b, proj_c, w_out)


def _ffn_kernel(x_ref, g_ref, wu_ref, cw_ref, cb_ref, wd_ref, fg_ref, o_ref, halo_scr,
                *, final_norm):
    t = pl.program_id(1)
    tm = x_ref.shape[1]
    x = x_ref[0]
    h = _rmsnorm(x, g_ref[...], EPS).astype(BF16)
    row = lax.broadcasted_iota(jnp.int32, (tm, FF_CHUNK), 0)

    @pl.when(t == 0)
    def _():
        halo_scr[...] = jnp.zeros(halo_scr.shape, F32)

    y = jnp.zeros((tm, D_MODEL), F32)
    for c in range(D_FF // FF_CHUNK):
        cols = slice(c * FF_CHUNK, (c + 1) * FF_CHUNK)
        gate = _dot(h, wu_ref[:, cols])
        up = _dot(h, wu_ref[:, D_FF + c * FF_CHUNK:D_FF + (c + 1) * FF_CHUNK])
        prev1 = halo_scr[7:8, cols]
        prev2 = halo_scr[6:7, cols]
        g1 = jnp.where(row == 0, prev1, pltpu.roll(gate, 1, 0))
        g2 = jnp.where(row == 0, prev2, jnp.where(row == 1, prev1, pltpu.roll(gate, 2, 0)))
        halo_scr[:, cols] = gate[tm - 8:tm, :]
        conv = (g2 * cw_ref[0:1, cols] + g1 * cw_ref[1:2, cols] + gate * cw_ref[2:3, cols]
                + cb_ref[:, cols])
        act = (_gelu(conv) * up).astype(BF16)
        y = y + _dot(act, wd_ref[cols, :])
    out = x + y
    if final_norm:
        out = _rmsnorm(out, fg_ref[...], EPS)
    o_ref[0] = out


def _ffn(x, g, w_up, conv_w, conv_b, w_down, final_g, final_norm):
    B, S, D = x.shape
    tm = TM_FFN
    tok = pl.BlockSpec((1, tm, D), lambda b, t: (b, t, 0))
    return pl.pallas_call(
        functools.partial(_ffn_kernel, final_norm=final_norm),
        grid=(B, S // tm),
        in_specs=[
            tok,
            _const_spec((1, D)),
            _const_spec((D, 2 * D_FF)),
            _const_spec((3, D_FF)),
            _const_spec((1, D_FF)),
            _const_spec((D_FF, D)),
            _const_spec((1, D)),
        ],
        out_specs=tok,
        out_shape=jax.ShapeDtypeStruct((B, S, D), F32),
        scratch_shapes=[pltpu.VMEM((8, D_FF), F32)],
        compiler_params=pltpu.CompilerParams(
            dimension_semantics=("arbitrary", "arbitrary"),
            vmem_limit_bytes=V7X_VMEM_LIMIT_BYTES),
        name="ffn_final" if final_norm else "ffn",
    )(x, g, w_up, conv_w, conv_b, w_down, final_g)


def kernel(x, attn_norm_g, w_in, w_gate, b_gate, sgu_ln_g, sgu_ln_b, sgu_w, sgu_b, proj_a, pool_w,
           pool_scale, proj_b, diff_lam, diff_subln_g, proj_c, w_out, ffn_norm_g, w_up, conv_w,
           conv_b, w_down, rel_bias, final_norm_g):
    B, S, D = x.shape
    L = DEPTH
    row = lambda a: a.reshape(a.shape[0], 1, a.shape[1])

    w_in_b = w_in.astype(BF16)
    w_gate_b = w_gate.astype(BF16)
    proj_a_b = proj_a.astype(BF16)
    proj_b_b = proj_b.astype(BF16)
    proj_c_b = proj_c.astype(BF16)
    w_out_b = w_out.astype(BF16)
    w_up_b = w_up.astype(BF16)
    w_down_b = w_down.astype(BF16)
    sgu_w_cat = jnp.transpose(sgu_w, (0, 2, 1, 3)).reshape(L, CHUNK, A_GROUPS * CHUNK)
    sgu_b_tile = jnp.repeat(jnp.transpose(sgu_b, (0, 2, 1)), A_GDIM, axis=2)
    eye = jnp.eye(len(POOL_WINDOWS), dtype=pool_w.dtype)
    pool_w_bd = jnp.einsum('lgcd,gh->lgchd', pool_w, eye).reshape(L, B_WIDTH, B_WIDTH).astype(BF16)

    bias = _bias_tiles(rel_bias)

    for l in range(L):
        lam_init = 0.8 - 0.6 * math.exp(-0.3 * l)
        scalars = jnp.asarray([lam_init], F32)
        ya, yb, q, k, v = _in_proj(
            x, row(attn_norm_g)[l], w_in_b[l], row(sgu_ln_g)[l], row(sgu_ln_b)[l],
            sgu_w_cat[l], sgu_b_tile[l], pool_w_bd[l], row(pool_scale)[l])
        yc = _attention(scalars, q, k, v, bias, diff_lam[l], row(diff_subln_g)[l])
        x2 = _merge(
            x.reshape(B * S, D), ya.reshape(B * S, A_WIDTH), yb.reshape(B * S, B_WIDTH),
            yc.reshape(B * S, C_V_WIDTH), row(attn_norm_g)[l], w_gate_b[l], row(b_gate)[l],
            proj_a_b[l], proj_b_b[l], proj_c_b[l], w_out_b[l])
        x = _ffn(x2.reshape(B, S, D), row(ffn_norm_g)[l], w_up_b[l], conv_w[l], row(conv_b)[l],
                 w_down_b[l], final_norm_g.reshape(1, D), final_norm=(l == L - 1))
    return x
```

```python
import functools
import math

import numpy as np
import jax
import jax.numpy as jnp
from jax import lax
from jax.experimental import pallas as pl
from jax.experimental.pallas import tpu as pltpu

D_MODEL = 1024
DEPTH = 4
A_WIDTH = 256
A_GROUPS = 4
A_GDIM = A_WIDTH // A_GROUPS
CHUNK = 128
B_WIDTH = 256
POOL_WINDOWS = (2, 4, 8, 16)
B_GDIM = B_WIDTH // len(POOL_WINDOWS)
C_HEADS = 4
C_HEAD_DIM = 64
C_VDIM = 2 * C_HEAD_DIM
C_QK_WIDTH = C_HEADS * 2 * C_HEAD_DIM
C_V_WIDTH = C_HEADS * C_VDIM
N_BUCKETS = 32
MAX_EXACT = 16
MAX_DISTANCE = 128
D_FF = 2816
EPS = 1e-6
SUBLN_EPS = 1e-5
LN_EPS = 1e-5
MASK_VALUE = -1e30
LOG2_E = math.log2(math.e)

O_A = 2 * A_WIDTH
O_B = O_A + B_WIDTH
O_Q = O_B + C_QK_WIDTH
O_K = O_Q + C_QK_WIDTH
IN_COLS = O_K + C_V_WIDTH

V7X_VMEM_LIMIT_BYTES = 56 * 1024 * 1024
HALO = 16
TM_IN = 512
TM_MERGE = 512
TM_FFN = 512
FF_CHUNK = 256
TQ = 512
TK = 512

F32 = jnp.float32
BF16 = jnp.bfloat16


def _dot(a, b):
    return jnp.dot(a, b, preferred_element_type=F32)


def _gelu(x):
    return 0.5 * x * (1.0 + lax.erf(x * np.float32(math.sqrt(0.5))))


def _rmsnorm(x, g, eps):
    return x * lax.rsqrt(jnp.mean(x * x, axis=-1, keepdims=True) + eps) * g


def _const_spec(shape):
    nd = len(shape)
    return pl.BlockSpec(shape, lambda *_: (0,) * nd, pipeline_mode=pl.Buffered(1))


def _in_proj_kernel(x_ref, g_ref, w_ref, lng_ref, lnb_ref, sw_ref, sb_ref, pw_ref, ps_ref,
                    ya_ref, yb_ref, q_ref, k_ref, v_ref, zb_scr):
    t = pl.program_id(1)
    tm = x_ref.shape[1]
    h = _rmsnorm(x_ref[0], g_ref[...], EPS).astype(BF16)

    za = _gelu(_dot(h, w_ref[:, 0:O_A]))
    u = za[:, :A_WIDTH]
    v = za[:, A_WIDTH:]
    mu = jnp.mean(v, axis=-1, keepdims=True)
    vc = v - mu
    var = jnp.mean(vc * vc, axis=-1, keepdims=True)
    vn = (vc * lax.rsqrt(var + LN_EPS) * lng_ref[...] + lnb_ref[...]).astype(BF16)
    row = lax.broadcasted_iota(jnp.int32, (CHUNK, A_GROUPS * CHUNK), 0)
    col = lax.broadcasted_iota(jnp.int32, (CHUNK, A_GROUPS * CHUNK), 1)
    ws = jnp.where((col % CHUNK) <= row, sw_ref[...], 0.0).astype(BF16)
    grp = lax.broadcasted_iota(jnp.int32, (CHUNK, A_WIDTH), 1) // A_GDIM
    for c in range(tm // CHUNK):
        vch = vn[c * CHUNK:(c + 1) * CHUNK, :]
        rhs = jnp.concatenate(
            [jnp.where(grp == g, vch, jnp.zeros_like(vch)) for g in range(A_GROUPS)], axis=0)
        mixed = _dot(ws, rhs) + sb_ref[...]
        ya_ref[0, c * CHUNK:(c + 1) * CHUNK, :] = (
            u[c * CHUNK:(c + 1) * CHUNK, :] * mixed).astype(ya_ref.dtype)

    zb = _dot(h, w_ref[:, O_A:O_B])

    @pl.when(t == 0)
    def _():
        zb_scr[0:HALO, :] = jnp.zeros((HALO, B_WIDTH), F32)

    @pl.when(t > 0)
    def _():
        zb_scr[0:HALO, :] = zb_scr[tm:tm + HALO, :]

    zb_scr[HALO:HALO + tm, :] = zb
    pos = t * tm + lax.broadcasted_iota(jnp.int32, (tm, 1), 0)
    lane = lax.broadcasted_iota(jnp.int32, (tm, 128), 1)
    halves = []
    for half in range(2):
        cols = slice(half * 128, (half + 1) * 128)
        w_small, w_big = POOL_WINDOWS[2 * half], POOL_WINDOWS[2 * half + 1]
        acc = zb[:, cols]
        for k in range(1, w_small):
            acc = acc + zb_scr[HALO - k:HALO - k + tm, cols]
        s_small = acc
        for k in range(w_small, w_big):
            acc = acc + zb_scr[HALO - k:HALO - k + tm, cols]
        cnt_small = jnp.minimum(pos + 1, w_small).astype(F32)
        cnt_big = jnp.minimum(pos + 1, w_big).astype(F32)
        pooled = jnp.where(lane < B_GDIM, s_small / cnt_small, acc / cnt_big)
        halves.append(pooled - zb[:, cols])
    p = jnp.concatenate(halves, axis=1).astype(BF16)
    yb_ref[0] = (_dot(p, pw_ref[...]) * ps_ref[...]).astype(yb_ref.dtype)

    zq = _dot(h, w_ref[:, O_B:O_Q]) * np.float32(C_HEAD_DIM ** -0.5 * LOG2_E)
    first_half = (lax.broadcasted_iota(jnp.int32, (tm, C_QK_WIDTH), 1) % C_VDIM) < C_HEAD_DIM
    q_ref[0, 0] = jnp.where(first_half, zq, 0.0).astype(q_ref.dtype)
    q_ref[0, 1] = jnp.where(first_half, 0.0, zq).astype(q_ref.dtype)
    k_ref[0] = _dot(h, w_ref[:, O_Q:O_K]).astype(k_ref.dtype)
    v_ref[0] = _dot(h, w_ref[:, O_K:IN_COLS]).astype(v_ref.dtype)


def _in_proj(x, g, w_in, ln_g, ln_b, sgu_w_cat, sgu_b_tile, pool_w_bd, pool_scale):
    B, S, D = x.shape
    tm = TM_IN
    grid = (B, S // tm)
    tok = lambda width: pl.BlockSpec((1, tm, width), lambda b, t: (b, t, 0))
    out_shape = (
        jax.ShapeDtypeStruct((B, S, A_WIDTH), BF16),
        jax.ShapeDtypeStruct((B, S, B_WIDTH), BF16),
        jax.ShapeDtypeStruct((B, 2, S, C_QK_WIDTH), BF16),
        jax.ShapeDtypeStruct((B, S, C_QK_WIDTH), BF16),
        jax.ShapeDtypeStruct((B, S, C_V_WIDTH), BF16),
    )
    return pl.pallas_call(
        _in_proj_kernel,
        grid=grid,
        in_specs=[
            tok(D),
            _const_spec((1, D)),
            _const_spec((D, IN_COLS)),
            _const_spec((1, A_WIDTH)),
            _const_spec((1, A_WIDTH)),
            _const_spec((CHUNK, A_GROUPS * CHUNK)),
            _const_spec((CHUNK, A_WIDTH)),
            _const_spec((B_WIDTH, B_WIDTH)),
            _const_spec((1, B_WIDTH)),
        ],
        out_specs=(
            tok(A_WIDTH),
            tok(B_WIDTH),
            pl.BlockSpec((1, 2, tm, C_QK_WIDTH), lambda b, t: (b, 0, t, 0)),
            tok(C_QK_WIDTH),
            tok(C_V_WIDTH),
        ),
        out_shape=out_shape,
        scratch_shapes=[pltpu.VMEM((HALO + tm, B_WIDTH), F32)],
        compiler_params=pltpu.CompilerParams(
            dimension_semantics=("arbitrary", "arbitrary"),
            vmem_limit_bytes=V7X_VMEM_LIMIT_BYTES),
        name="in_proj",
    )(x, g, w_in, ln_g, ln_b, sgu_w_cat, sgu_b_tile, pool_w_bd, pool_scale)


def _bucket_tiles():
    r = np.arange(TQ)[:, None]
    c = np.arange(TK)[None, :]
    tiles = []
    for d in range(2):
        rel = d * TQ + r - c
        n = np.maximum(rel, 0)
        nf = np.maximum(n, 1).astype(np.float32)
        large = MAX_EXACT + (np.log(nf / MAX_EXACT) / math.log(MAX_DISTANCE / MAX_EXACT)
                             * (N_BUCKETS - MAX_EXACT)).astype(np.int32)
        large = np.minimum(large, N_BUCKETS - 1)
        bucket = np.where(n < MAX_EXACT, n, large)
        tiles.append(np.where(rel >= 0, bucket, -1))
    return np.stack(tiles).astype(np.int32)


def _bias_kernel(rb_ref, bk_ref, o_ref):
    for h in range(C_HEADS):
        far = rb_ref[N_BUCKETS - 1, h]
        for d in range(2):
            bk = bk_ref[d]
            acc = jnp.zeros(bk.shape, F32)
            for b in range(N_BUCKETS - 1):
                acc = jnp.where(bk == b, (rb_ref[b, h] - far) * np.float32(LOG2_E), acc)
            o_ref[h, d] = jnp.where(bk < 0, np.float32(MASK_VALUE), acc)


def _bias_tiles(rel_bias):
    buckets = jnp.asarray(_bucket_tiles())
    return pl.pallas_call(
        _bias_kernel,
        in_specs=[pl.BlockSpec(memory_space=pltpu.SMEM),
                  pl.BlockSpec(memory_space=pltpu.VMEM)],
        out_specs=pl.BlockSpec(memory_space=pltpu.VMEM),
        out_shape=jax.ShapeDtypeStruct((C_HEADS, 2, TQ, TK), F32),
        name="bias_tiles",
    )(rel_bias, buckets)


def _attn_kernel(sc_ref, q_ref, k_ref, v_ref, bias_ref, lamv_ref, g_ref, o_ref,
                 m_scr, l_scr, acc_scr):
    qi = pl.program_id(1)
    lam_init = sc_ref[0]
    lv = lamv_ref[...]
    lam = (jnp.exp(jnp.sum(lv[0:1] * lv[1:2], axis=-1, keepdims=True))
           - jnp.exp(jnp.sum(lv[2:3] * lv[3:4], axis=-1, keepdims=True)) + lam_init)

    m_scr[...] = jnp.full(m_scr.shape, -jnp.inf, F32)
    l_scr[...] = jnp.zeros(l_scr.shape, F32)
    acc_scr[...] = jnp.zeros(acc_scr.shape, F32)

    def step(j, bias_idx):
        start = pl.multiple_of(j * TK, TK)
        for h in range(C_HEADS):
            cols = slice(h * C_VDIM, (h + 1) * C_VDIM)
            qs = jnp.concatenate([q_ref[0, 0, :, cols], q_ref[0, 1, :, cols]], axis=0)
            kt = k_ref[0, pl.ds(start, TK), cols]
            vt = v_ref[0, pl.ds(start, TK), cols]
            s = lax.dot_general(qs, kt, (((1,), (1,)), ((), ())), preferred_element_type=F32)
            if bias_idx is not None:
                bias = bias_ref[h, bias_idx]
                s = s + jnp.concatenate([bias, bias], axis=0)
            m_prev = m_scr[h]
            m_new = jnp.maximum(m_prev, jnp.max(s, axis=-1, keepdims=True))
            alpha = jnp.exp2(m_prev - m_new)
            p = jnp.exp2(s - jnp.concatenate([m_new] * (TK // 128), axis=1))
            l_scr[h] = alpha * l_scr[h] + jnp.sum(p, axis=-1, keepdims=True)
            acc_scr[h] = alpha * acc_scr[h] + _dot(p.astype(BF16), vt)
            m_scr[h] = m_new

    def plain_step(j, carry):
        step(j, None)
        return carry

    lax.fori_loop(0, jnp.maximum(qi - 1, 0), plain_step, 0)

    @pl.when(qi >= 1)
    def _():
        step(qi - 1, 1)

    step(qi, 0)

    for h in range(C_HEADS):
        cols = slice(h * C_VDIM, (h + 1) * C_VDIM)
        o_all = acc_scr[h] / l_scr[h]
        o = o_all[:TQ] - lam * o_all[TQ:]
        o = _rmsnorm(o, g_ref[...], SUBLN_EPS) * (1.0 - lam_init)
        o_ref[0, :, cols] = o.astype(o_ref.dtype)


def _attention(scalars, q, k, v, bias, lam_vecs, subln_g):
    B, S, _ = k.shape
    grid = (B, S // TQ)
    return pl.pallas_call(
        _attn_kernel,
        grid=grid,
        in_specs=[
            pl.BlockSpec(memory_space=pltpu.SMEM),
            pl.BlockSpec((1, 2, TQ, C_QK_WIDTH), lambda b, i: (b, 0, i, 0)),
            pl.BlockSpec((1, S, C_QK_WIDTH), lambda b, i: (b, 0, 0)),
            pl.BlockSpec((1, S, C_V_WIDTH), lambda b, i: (b, 0, 0)),
            _const_spec((C_HEADS, 2, TQ, TK)),
            _const_spec((4, C_HEAD_DIM)),
            _const_spec((1, C_VDIM)),
        ],
        out_specs=pl.BlockSpec((1, TQ, C_V_WIDTH), lambda b, i: (b, i, 0)),
        out_shape=jax.ShapeDtypeStruct((B, S, C_V_WIDTH), BF16),
        scratch_shapes=[
            pltpu.VMEM((C_HEADS, 2 * TQ, 128), F32),
            pltpu.VMEM((C_HEADS, 2 * TQ, 128), F32),
            pltpu.VMEM((C_HEADS, 2 * TQ, C_VDIM), F32),
        ],
        compiler_params=pltpu.CompilerParams(
            dimension_semantics=("arbitrary", "arbitrary"),
            vmem_limit_bytes=V7X_VMEM_LIMIT_BYTES),
        name="diff_attn",
    )(scalars, q, k, v, bias, lam_vecs, subln_g)


def _merge_kernel(x_ref, ya_ref, yb_ref, yc_ref, g_ref, wg_ref, bg_ref,
                  pa_ref, pb_ref, pc_ref, wo_ref, o_ref):
    x = x_ref[...]
    h = _rmsnorm(x, g_ref[...], EPS).astype(BF16)
    merged = None
    for i, (y_ref, p_ref) in enumerate(((ya_ref, pa_ref), (yb_ref, pb_ref), (yc_ref, pc_ref))):
        cols = slice(i * D_MODEL, (i + 1) * D_MODEL)
        pre = _dot(h, wg_ref[:, cols]) + bg_ref[:, cols]
        gate = 1.0 / (1.0 + jnp.exp(-pre))
        term = gate * _dot(y_ref[...], p_ref[...])
        merged = term if merged is None else merged + term
    o_ref[...] = x + _dot(merged.astype(BF16), wo_ref[...])


def _merge(x2, ya, yb, yc, g, w_gate, b_gate, proj_a, proj_b, proj_c, w_out):
    N, D = x2.shape
    tm = TM_MERGE
    tok = lambda width: pl.BlockSpec((tm, width), lambda t: (t, 0))
    return pl.pallas_call(
        _merge_kernel,
        grid=(N // tm,),
        in_specs=[
            tok(D), tok(A_WIDTH), tok(B_WIDTH), tok(C_V_WIDTH),
            _const_spec((1, D)),
            _const_spec((D, 3 * D)),
            _const_spec((1, 3 * D)),
            _const_spec((A_WIDTH, D)),
            _const_spec((B_WIDTH, D)),
            _const_spec((C_V_WIDTH, D)),
            _const_spec((D, D)),
        ],
        out_specs=tok(D),
        out_shape=jax.ShapeDtypeStruct((N, D), F32),
        compiler_params=pltpu.CompilerParams(
            dimension_semantics=("arbitrary",),
            vmem_limit_bytes=V7X_VMEM_LIMIT_BYTES),
        name="merge",
    )(x2, ya, yb, yc, g, w_gate, b_gate, proj_a, proj_b, proj_c, w_out)


def _ffn_kernel(x_ref, g_ref, wu_ref, cw_ref, cb_ref, wd_ref, fg_ref, o_ref, halo_scr,
                *, final_norm):
    t = pl.program_id(1)
    tm = x_ref.shape[1]
    x = x_ref[0]
    h = _rmsnorm(x, g_ref[...], EPS).astype(BF16)
    row = lax.broadcasted_iota(jnp.int32, (tm, FF_CHUNK), 0)

    @pl.when(t == 0)
    def _():
        halo_scr[...] = jnp.zeros(halo_scr.shape, F32)

    y = jnp.zeros((tm, D_MODEL), F32)
    for c in range(D_FF // FF_CHUNK):
        cols = slice(c * FF_CHUNK, (c + 1) * FF_CHUNK)
        gate = _dot(h, wu_ref[:, cols])
        up = _dot(h, wu_ref[:, D_FF + c * FF_CHUNK:D_FF + (c + 1) * FF_CHUNK])
        prev1 = halo_scr[7:8, cols]
        prev2 = halo_scr[6:7, cols]
        g1 = jnp.where(row == 0, prev1, pltpu.roll(gate, 1, 0))
        g2 = jnp.where(row == 0, prev2, jnp.where(row == 1, prev1, pltpu.roll(gate, 2, 0)))
        halo_scr[:, cols] = gate[tm - 8:tm, :]
        conv = (g2 * cw_ref[0:1, cols] + g1 * cw_ref[1:2, cols] + gate * cw_ref[2:3, cols]
                + cb_ref[:, cols])
        act = (_gelu(conv) * up).astype(BF16)
        y = y + _dot(act, wd_ref[cols, :])
    out = x + y
    if final_norm:
        out = _rmsnorm(out, fg_ref[...], EPS)
    o_ref[0] = out


def _ffn(x, g, w_up, conv_w, conv_b, w_down, final_g, final_norm):
    B, S, D = x.shape
    tm = TM_FFN
    tok = pl.BlockSpec((1, tm, D), lambda b, t: (b, t, 0))
    return pl.pallas_call(
        functools.partial(_ffn_kernel, final_norm=final_norm),
        grid=(B, S // tm),
        in_specs=[
            tok,
            _const_spec((1, D)),
            _const_spec((D, 2 * D_FF)),
            _const_spec((3, D_FF)),
            _const_spec((1, D_FF)),
            _const_spec((D_FF, D)),
            _const_spec((1, D)),
        ],
        out_specs=tok,
        out_shape=jax.ShapeDtypeStruct((B, S, D), F32),
        scratch_shapes=[pltpu.VMEM((8, D_FF), F32)],
        compiler_params=pltpu.CompilerParams(
            dimension_semantics=("arbitrary", "arbitrary"),
            vmem_limit_bytes=V7X_VMEM_LIMIT_BYTES),
        name="ffn_final" if final_norm else "ffn",
    )(x, g, w_up, conv_w, conv_b, w_down, final_g)


def kernel(x, attn_norm_g, w_in, w_gate, b_gate, sgu_ln_g, sgu_ln_b, sgu_w, sgu_b, proj_a, pool_w,
           pool_scale, proj_b, diff_lam, diff_subln_g, proj_c, w_out, ffn_norm_g, w_up, conv_w,
           conv_b, w_down, rel_bias, final_norm_g):
    B, S, D = x.shape
    L = DEPTH
    row = lambda a: a.reshape(a.shape[0], 1, a.shape[1])

    w_in_b = w_in.astype(BF16)
    w_gate_b = w_gate.astype(BF16)
    proj_a_b = proj_a.astype(BF16)
    proj_b_b = proj_b.astype(BF16)
    proj_c_b = proj_c.astype(BF16)
    w_out_b = w_out.astype(BF16)
    w_up_b = w_up.astype(BF16)
    w_down_b = w_down.astype(BF16)
    sgu_w_cat = jnp.transpose(sgu_w, (0, 2, 1, 3)).reshape(L, CHUNK, A_GROUPS * CHUNK)
    sgu_b_tile = jnp.repeat(jnp.transpose(sgu_b, (0, 2, 1)), A_GDIM, axis=2)
    eye = jnp.eye(len(POOL_WINDOWS), dtype=pool_w.dtype)
    pool_w_bd = jnp.einsum('lgcd,gh->lgchd', pool_w, eye).reshape(L, B_WIDTH, B_WIDTH).astype(BF16)

    bias = _bias_tiles(rel_bias)

    for l in range(L):
        lam_init = 0.8 - 0.6 * math.exp(-0.3 * l)
        scalars = jnp.asarray([lam_init], F32)
        ya, yb, q, k, v = _in_proj(
            x, row(attn_norm_g)[l], w_in_b[l], row(sgu_ln_g)[l], row(sgu_ln_b)[l],
            sgu_w_cat[l], sgu_b_tile[l], pool_w_bd[l], row(pool_scale)[l])
        yc = _attention(scalars, q, k, v, bias, diff_lam[l], row(diff_subln_g)[l])
        x2 = _merge(
            x.reshape(B * S, D), ya.reshape(B * S, A_WIDTH), yb.reshape(B * S, B_WIDTH),
            yc.reshape(B * S, C_V_WIDTH), row(attn_norm_g)[l], w_gate_b[l], row(b_gate)[l],
            proj_a_b[l], proj_b_b[l], proj_c_b[l], w_out_b[l])
        x = _ffn(x2.reshape(B, S, D), row(ffn_norm_g)[l], w_up_b[l], conv_w[l], row(conv_b)[l],
                 w_down_b[l], final_norm_g.reshape(1, D), final_norm=(l == L - 1))
    return x
```

```python
import functools
import math

import numpy as np
import jax
import jax.numpy as jnp
from jax import lax
from jax.experimental import pallas as pl
from jax.experimental.pallas import tpu as pltpu

D_MODEL = 1024
DEPTH = 4
A_WIDTH = 256
A_GROUPS = 4
A_GDIM = A_WIDTH // A_GROUPS
CHUNK = 128
B_WIDTH = 256
POOL_WINDOWS = (2, 4, 8, 16)
B_GDIM = B_WIDTH // len(POOL_WINDOWS)
C_HEADS = 4
C_HEAD_DIM = 64
C_VDIM = 2 * C_HEAD_DIM
C_QK_WIDTH = C_HEADS * 2 * C_HEAD_DIM
C_V_WIDTH = C_HEADS * C_VDIM
N_BUCKETS = 32
MAX_EXACT = 16
MAX_DISTANCE = 128
D_FF = 2816
EPS = 1e-6
SUBLN_EPS = 1e-5
LN_EPS = 1e-5
MASK_VALUE = -1e30
LOG2_E = math.log2(math.e)

O_A = 2 * A_WIDTH
O_B = O_A + B_WIDTH
O_Q = O_B + C_QK_WIDTH
O_K = O_Q + C_QK_WIDTH
IN_COLS = O_K + C_V_WIDTH

V7X_VMEM_LIMIT_BYTES = 56 * 1024 * 1024
HALO = 16
TM_IN = 512
TM_MERGE = 512
TM_FFN = 512
FF_CHUNK = 256
TQ = 512
TK = 512
QB = 256

F32 = jnp.float32
BF16 = jnp.bfloat16


def _dot(a, b):
    return jnp.dot(a, b, preferred_element_type=F32)


def _gelu(x):
    return 0.5 * x * (1.0 + lax.erf(x * np.float32(math.sqrt(0.5))))


def _rmsnorm(x, g, eps):
    return x * lax.rsqrt(jnp.mean(x * x, axis=-1, keepdims=True) + eps) * g


def _const_spec(shape):
    nd = len(shape)
    return pl.BlockSpec(shape, lambda *_: (0,) * nd, pipeline_mode=pl.Buffered(1))


def _in_proj_kernel(x_ref, g_ref, w_ref, lng_ref, lnb_ref, sw_ref, sb_ref, pw_ref, ps_ref,
                    ya_ref, yb_ref, q_ref, k_ref, vt_ref, zb_scr):
    t = pl.program_id(1)
    tm = x_ref.shape[1]
    h = _rmsnorm(x_ref[0], g_ref[...], EPS).astype(BF16)

    za = _dot(h, w_ref[:, 0:O_A])
    zb = _dot(h, w_ref[:, O_A:O_B])
    k_ref[0] = _dot(h, w_ref[:, O_Q:O_K]).astype(k_ref.dtype)

    za = _gelu(za)
    u = za[:, :A_WIDTH]
    v = za[:, A_WIDTH:]
    mu = jnp.mean(v, axis=-1, keepdims=True)
    vc = v - mu
    var = jnp.mean(vc * vc, axis=-1, keepdims=True)
    vn = (vc * lax.rsqrt(var + LN_EPS) * lng_ref[...] + lnb_ref[...]).astype(BF16)

    vt_ref[0, 0] = _dot(h, w_ref[:, O_K:IN_COLS]).T.astype(vt_ref.dtype)

    row = lax.broadcasted_iota(jnp.int32, (CHUNK, A_GROUPS * CHUNK), 0)
    col = lax.broadcasted_iota(jnp.int32, (CHUNK, A_GROUPS * CHUNK), 1)
    ws = jnp.where((col % CHUNK) <= row, sw_ref[...], 0.0).astype(BF16)
    grp = lax.broadcasted_iota(jnp.int32, (CHUNK, A_WIDTH), 1) // A_GDIM
    for c in range(tm // CHUNK):
        vch = vn[c * CHUNK:(c + 1) * CHUNK, :]
        rhs = jnp.concatenate(
            [jnp.where(grp == g, vch, jnp.zeros_like(vch)) for g in range(A_GROUPS)], axis=0)
        mixed = _dot(ws, rhs) + sb_ref[...]
        ya_ref[0, c * CHUNK:(c + 1) * CHUNK, :] = (
            u[c * CHUNK:(c + 1) * CHUNK, :] * mixed).astype(ya_ref.dtype)

    zq = _dot(h, w_ref[:, O_B:O_Q]) * np.float32(C_HEAD_DIM ** -0.5 * LOG2_E)
    first_half = (lax.broadcasted_iota(jnp.int32, (tm, C_QK_WIDTH), 1) % C_VDIM) < C_HEAD_DIM
    q_ref[0, 0] = jnp.where(first_half, zq, 0.0).astype(q_ref.dtype)
    q_ref[0, 1] = jnp.where(first_half, 0.0, zq).astype(q_ref.dtype)

    @pl.when(t == 0)
    def _():
        zb_scr[0:HALO, :] = jnp.zeros((HALO, B_WIDTH), F32)

    @pl.when(t > 0)
    def _():
        zb_scr[0:HALO, :] = zb_scr[tm:tm + HALO, :]

    zb_scr[HALO:HALO + tm, :] = zb
    pos = t * tm + lax.broadcasted_iota(jnp.int32, (tm, 1), 0)
    lane = lax.broadcasted_iota(jnp.int32, (tm, 128), 1)
    halves = []
    for half in range(2):
        cols = slice(half * 128, (half + 1) * 128)
        w_small, w_big = POOL_WINDOWS[2 * half], POOL_WINDOWS[2 * half + 1]
        acc = zb[:, cols]
        for k in range(1, w_small):
            acc = acc + zb_scr[HALO - k:HALO - k + tm, cols]
        s_small = acc
        for k in range(w_small, w_big):
            acc = acc + zb_scr[HALO - k:HALO - k + tm, cols]
        cnt_small = jnp.minimum(pos + 1, w_small).astype(F32)
        cnt_big = jnp.minimum(pos + 1, w_big).astype(F32)
        pooled = jnp.where(lane < B_GDIM, s_small / cnt_small, acc / cnt_big)
        halves.append(pooled - zb[:, cols])
    p = jnp.concatenate(halves, axis=1).astype(BF16)
    yb_ref[0] = (_dot(p, pw_ref[...]) * ps_ref[...]).astype(yb_ref.dtype)


def _in_proj(x, g, w_in, ln_g, ln_b, sgu_w_cat, sgu_b_tile, pool_w_bd, pool_scale):
    B, S, D = x.shape
    tm = TM_IN
    assert tm == TK, "the transposed value tiles are written one attention key tile at a time"
    grid = (B, S // tm)
    tok = lambda width: pl.BlockSpec((1, tm, width), lambda b, t: (b, t, 0))
    out_shape = (
        jax.ShapeDtypeStruct((B, S, A_WIDTH), BF16),
        jax.ShapeDtypeStruct((B, S, B_WIDTH), BF16),
        jax.ShapeDtypeStruct((B, 2, S, C_QK_WIDTH), BF16),
        jax.ShapeDtypeStruct((B, S, C_QK_WIDTH), BF16),
        jax.ShapeDtypeStruct((B, S // tm, C_V_WIDTH, tm), BF16),
    )
    return pl.pallas_call(
        _in_proj_kernel,
        grid=grid,
        in_specs=[
            tok(D),
            _const_spec((1, D)),
            _const_spec((D, IN_COLS)),
            _const_spec((1, A_WIDTH)),
            _const_spec((1, A_WIDTH)),
            _const_spec((CHUNK, A_GROUPS * CHUNK)),
            _const_spec((CHUNK, A_WIDTH)),
            _const_spec((B_WIDTH, B_WIDTH)),
            _const_spec((1, B_WIDTH)),
        ],
        out_specs=(
            tok(A_WIDTH),
            tok(B_WIDTH),
            pl.BlockSpec((1, 2, tm, C_QK_WIDTH), lambda b, t: (b, 0, t, 0)),
            tok(C_QK_WIDTH),
            pl.BlockSpec((1, 1, C_V_WIDTH, tm), lambda b, t: (b, t, 0, 0)),
        ),
        out_shape=out_shape,
        scratch_shapes=[pltpu.VMEM((HALO + tm, B_WIDTH), F32)],
        compiler_params=pltpu.CompilerParams(
            dimension_semantics=("arbitrary", "arbitrary"),
            vmem_limit_bytes=V7X_VMEM_LIMIT_BYTES),
        name="in_proj",
    )(x, g, w_in, ln_g, ln_b, sgu_w_cat, sgu_b_tile, pool_w_bd, pool_scale)


def _bucket_tiles():
    r = np.arange(TQ)[None, :]
    c = np.arange(TK)[:, None]
    tiles = []
    for d in range(2):
        rel = d * TQ + r - c
        n = np.maximum(rel, 0)
        nf = np.maximum(n, 1).astype(np.float32)
        large = MAX_EXACT + (np.log(nf / MAX_EXACT) / math.log(MAX_DISTANCE / MAX_EXACT)
                             * (N_BUCKETS - MAX_EXACT)).astype(np.int32)
        large = np.minimum(large, N_BUCKETS - 1)
        bucket = np.where(n < MAX_EXACT, n, large)
        tiles.append(np.where(rel >= 0, bucket, -1))
    return np.stack(tiles).astype(np.int32)


def _bias_kernel(rb_ref, bk_ref, o_ref):
    for h in range(C_HEADS):
        far = rb_ref[N_BUCKETS - 1, h]
        for d in range(2):
            bk = bk_ref[d]
            acc = jnp.zeros(bk.shape, F32)
            for b in range(N_BUCKETS - 1):
                acc = jnp.where(bk == b, (rb_ref[b, h] - far) * np.float32(LOG2_E), acc)
            o_ref[h, d] = jnp.where(bk < 0, np.float32(MASK_VALUE), acc)


def _bias_tiles(rel_bias):
    buckets = jnp.asarray(_bucket_tiles())
    return pl.pallas_call(
        _bias_kernel,
        in_specs=[pl.BlockSpec(memory_space=pltpu.SMEM),
                  pl.BlockSpec(memory_space=pltpu.VMEM)],
        out_specs=pl.BlockSpec(memory_space=pltpu.VMEM),
        out_shape=jax.ShapeDtypeStruct((C_HEADS, 2, TK, TQ), F32),
        name="bias_tiles",
    )(rel_bias, buckets)


def _visible_keys(tile_offset, b):
    if tile_offset == 0:
        return min(TK, (b + 1) * QB)
    return TK


def _first_biased_key(tile_offset, b):
    if tile_offset is None:
        return TK
    first = tile_offset * TQ + b * QB - MAX_DISTANCE + 1
    return min(TK, max(0, first // 8 * 8))


def _attn_kernel(sc_ref, q_ref, k_ref, vt_ref, bias_ref, lamv_ref, g_ref, o_ref,
                 m_scr, l_scr, acc_scr):
    qi = pl.program_id(1)
    lam_init = sc_ref[0]
    lv = lamv_ref[...]
    lam = (jnp.exp(jnp.sum(lv[0:1] * lv[1:2], axis=-1, keepdims=True))
           - jnp.exp(jnp.sum(lv[2:3] * lv[3:4], axis=-1, keepdims=True)) + lam_init)

    m_scr[...] = jnp.full(m_scr.shape, -jnp.inf, F32)
    l_scr[...] = jnp.zeros(l_scr.shape, F32)
    acc_scr[...] = jnp.zeros(acc_scr.shape, F32)

    def step(j, bias_idx):
        start = pl.multiple_of(j * TK, TK)
        blocks = [(h, c, b) for h in range(C_HEADS) for c in range(2) for b in range(TQ // QB)]
        scores = {}
        for h, c, b in blocks:
            cols = slice(h * C_VDIM, (h + 1) * C_VDIM)
            nk = _visible_keys(bias_idx, b)
            kt = k_ref[0, pl.ds(start, nk), cols]
            qb = q_ref[0, c, b * QB:(b + 1) * QB, cols]
            s = lax.dot_general(
                kt, qb, (((1,), (1,)), ((), ())), preferred_element_type=F32)
            lo = _first_biased_key(bias_idx, b)
            if lo < nk:
                biased = s[lo:] + bias_ref[h, bias_idx, lo:nk, b * QB:(b + 1) * QB]
                s = biased if lo == 0 else jnp.concatenate([s[:lo], biased], axis=0)
            scores[h, c, b] = s
        for h, c, b in blocks:
            cols = slice(h * C_VDIM, (h + 1) * C_VDIM)
            qsl = slice(c * TQ + b * QB, c * TQ + (b + 1) * QB)
            nk = _visible_keys(bias_idx, b)
            vt = vt_ref[0, j, cols, 0:nk]
            s = scores[h, c, b]
            m_prev = m_scr[h, :, qsl]
            m_new = jnp.maximum(m_prev, jnp.max(s, axis=0, keepdims=True))
            alpha = jnp.exp2(m_prev - m_new)
            p = jnp.exp2(s - m_new)
            l_scr[h, :, qsl] = alpha * l_scr[h, :, qsl] + jnp.sum(p, axis=0, keepdims=True)
            acc_scr[h, :, qsl] = alpha * acc_scr[h, :, qsl] + _dot(vt, p.astype(BF16))
            m_scr[h, :, qsl] = m_new

    def plain_step(j, carry):
        step(j, None)
        return carry

    lax.fori_loop(0, jnp.maximum(qi - 1, 0), plain_step, 0)

    @pl.when(qi >= 1)
    def _():
        step(qi - 1, 1)

    step(qi, 0)

    for h in range(C_HEADS):
        cols = slice(h * C_VDIM, (h + 1) * C_VDIM)
        o_all = acc_scr[h] / l_scr[h]
        o = o_all[:, :TQ] - lam * o_all[:, TQ:]
        o = o * lax.rsqrt(jnp.mean(o * o, axis=0, keepdims=True) + SUBLN_EPS)
        o = o * g_ref[...] * (1.0 - lam_init)
        o_ref[0, :, cols] = o.T.astype(o_ref.dtype)


def _attention(scalars, q, k, vt, bias, lam_vecs, subln_g):
    B, S, _ = k.shape
    grid = (B, S // TQ)
    return pl.pallas_call(
        _attn_kernel,
        grid=grid,
        in_specs=[
            pl.BlockSpec(memory_space=pltpu.SMEM),
            pl.BlockSpec((1, 2, TQ, C_QK_WIDTH), lambda b, i: (b, 0, i, 0)),
            pl.BlockSpec((1, S, C_QK_WIDTH), lambda b, i: (b, 0, 0)),
            pl.BlockSpec((1, S // TK, C_V_WIDTH, TK), lambda b, i: (b, 0, 0, 0)),
            _const_spec((C_HEADS, 2, TK, TQ)),
            _const_spec((4, C_HEAD_DIM)),
            _const_spec((C_VDIM, 1)),
        ],
        out_specs=pl.BlockSpec((1, TQ, C_V_WIDTH), lambda b, i: (b, i, 0)),
        out_shape=jax.ShapeDtypeStruct((B, S, C_V_WIDTH), BF16),
        scratch_shapes=[
            pltpu.VMEM((C_HEADS, 1, 2 * TQ), F32),
            pltpu.VMEM((C_HEADS, 1, 2 * TQ), F32),
            pltpu.VMEM((C_HEADS, C_VDIM, 2 * TQ), F32),
        ],
        compiler_params=pltpu.CompilerParams(
            dimension_semantics=("arbitrary", "arbitrary"),
            vmem_limit_bytes=V7X_VMEM_LIMIT_BYTES),
        name="diff_attn",
    )(scalars, q, k, vt, bias, lam_vecs, subln_g)


def _merge_kernel(x_ref, ya_ref, yb_ref, yc_ref, g_ref, wg_ref, bg_ref,
                  pa_ref, pb_ref, pc_ref, wo_ref, o_ref):
    x = x_ref[...]
    h = _rmsnorm(x, g_ref[...], EPS).astype(BF16)
    merged = None
    for i, (y_ref, p_ref) in enumerate(((ya_ref, pa_ref), (yb_ref, pb_ref), (yc_ref, pc_ref))):
        cols = slice(i * D_MODEL, (i + 1) * D_MODEL)
        pre = _dot(h, wg_ref[:, cols]) + bg_ref[:, cols]
        gate = 1.0 / (1.0 + jnp.exp(-pre))
        term = gate * _dot(y_ref[...], p_ref[...])
        merged = term if merged is None else merged + term
    o_ref[...] = x + _dot(merged.astype(BF16), wo_ref[...])


def _merge(x2, ya, yb, yc, g, w_gate, b_gate, proj_a, proj_b, proj_c, w_out):
    N, D = x2.shape
    tm = TM_MERGE
    tok = lambda width: pl.BlockSpec((tm, width), lambda t: (t, 0))
    return pl.pallas_call(
        _merge_kernel,
        grid=(N // tm,),
        in_specs=[
            tok(D), tok(A_WIDTH), tok(B_WIDTH), tok(C_V_WIDTH),
            _const_spec((1, D)),
            _const_spec((D, 3 * D)),
            _const_spec((1, 3 * D)),
            _const_spec((A_WIDTH, D)),
            _const_spec((B_WIDTH, D)),
            _const_spec((C_V_WIDTH, D)),
            _const_spec((D, D)),
        ],
        out_specs=tok(D),
        out_shape=jax.ShapeDtypeStruct((N, D), F32),
        compiler_params=pltpu.CompilerParams(
            dimension_semantics=("arbitrary",),
            vmem_limit_bytes=V7X_VMEM_LIMIT_BYTES),
        name="merge",
    )(x2, ya, yb, yc, g, w_gate, b_gate, proj_a, proj_b, proj_c, w_out)


def _ffn_kernel(x_ref, g_ref, wu_ref, cw_ref, cb_ref, wd_ref, fg_ref, o_ref, halo_scr, act_scr,
                *, final_norm):
    t = pl.program_id(1)
    tm = x_ref.shape[1]
    x = x_ref[0]
    h = _rmsnorm(x, g_ref[...], EPS).astype(BF16)
    row = lax.broadcasted_iota(jnp.int32, (tm, FF_CHUNK), 0)

    @pl.when(t == 0)
    def _():
        halo_scr[...] = jnp.zeros(halo_scr.shape, F32)

    for c in range(D_FF // FF_CHUNK):
        cols = slice(c * FF_CHUNK, (c + 1) * FF_CHUNK)
        gate = _dot(h, wu_ref[:, cols])
        up = _dot(h, wu_ref[:, D_FF + c * FF_CHUNK:D_FF + (c + 1) * FF_CHUNK])
        prev1 = halo_scr[7:8, cols]
        prev2 = halo_scr[6:7, cols]
        g1 = jnp.where(row == 0, prev1, pltpu.roll(gate, 1, 0))
        g2 = jnp.where(row == 0, prev2, jnp.where(row == 1, prev1, pltpu.roll(gate, 2, 0)))
        halo_scr[:, cols] = gate[tm - 8:tm, :]
        conv = (g2 * cw_ref[0:1, cols] + g1 * cw_ref[1:2, cols] + gate * cw_ref[2:3, cols]
                + cb_ref[:, cols])
        act_scr[:, cols] = (_gelu(conv) * up).astype(BF16)
    out = x + _dot(act_scr[...], wd_ref[...])
    if final_norm:
        out = _rmsnorm(out, fg_ref[...], EPS)
    o_ref[0] = out


def _ffn(x, g, w_up, conv_w, conv_b, w_down, final_g, final_norm):
    B, S, D = x.shape
    tm = TM_FFN
    tok = pl.BlockSpec((1, tm, D), lambda b, t: (b, t, 0))
    return pl.pallas_call(
        functools.partial(_ffn_kernel, final_norm=final_norm),
        grid=(B, S // tm),
        in_specs=[
            tok,
            _const_spec((1, D)),
            _const_spec((D, 2 * D_FF)),
            _const_spec((3, D_FF)),
            _const_spec((1, D_FF)),
            _const_spec((D_FF, D)),
            _const_spec((1, D)),
        ],
        out_specs=tok,
        out_shape=jax.ShapeDtypeStruct((B, S, D), F32),
        scratch_shapes=[pltpu.VMEM((8, D_FF), F32), pltpu.VMEM((tm, D_FF), BF16)],
        compiler_params=pltpu.CompilerParams(
            dimension_semantics=("arbitrary", "arbitrary"),
            vmem_limit_bytes=V7X_VMEM_LIMIT_BYTES),
        name="ffn_final" if final_norm else "ffn",
    )(x, g, w_up, conv_w, conv_b, w_down, final_g)


def kernel(x, attn_norm_g, w_in, w_gate, b_gate, sgu_ln_g, sgu_ln_b, sgu_w, sgu_b, proj_a, pool_w,
           pool_scale, proj_b, diff_lam, diff_subln_g, proj_c, w_out, ffn_norm_g, w_up, conv_w,
           conv_b, w_down, rel_bias, final_norm_g):
    B, S, D = x.shape
    L = DEPTH
    row = lambda a: a.reshape(a.shape[0], 1, a.shape[1])

    w_in_b = w_in.astype(BF16)
    w_gate_b = w_gate.astype(BF16)
    proj_a_b = proj_a.astype(BF16)
    proj_b_b = proj_b.astype(BF16)
    proj_c_b = proj_c.astype(BF16)
    w_out_b = w_out.astype(BF16)
    w_up_b = w_up.astype(BF16)
    w_down_b = w_down.astype(BF16)
    sgu_w_cat = jnp.transpose(sgu_w, (0, 2, 1, 3)).reshape(L, CHUNK, A_GROUPS * CHUNK)
    sgu_b_tile = jnp.repeat(jnp.transpose(sgu_b, (0, 2, 1)), A_GDIM, axis=2)
    eye = jnp.eye(len(POOL_WINDOWS), dtype=pool_w.dtype)
    pool_w_bd = jnp.einsum('lgcd,gh->lgchd', pool_w, eye).reshape(L, B_WIDTH, B_WIDTH).astype(BF16)

    bias = _bias_tiles(rel_bias)

    for l in range(L):
        lam_init = 0.8 - 0.6 * math.exp(-0.3 * l)
        scalars = jnp.asarray([lam_init], F32)
        ya, yb, q, k, vt = _in_proj(
            x, row(attn_norm_g)[l], w_in_b[l], row(sgu_ln_g)[l], row(sgu_ln_b)[l],
            sgu_w_cat[l], sgu_b_tile[l], pool_w_bd[l], row(pool_scale)[l])
        yc = _attention(scalars, q, k, vt, bias, diff_lam[l], diff_subln_g[l].reshape(C_VDIM, 1))
        x2 = _merge(
            x.reshape(B * S, D), ya.reshape(B * S, A_WIDTH), yb.reshape(B * S, B_WIDTH),
            yc.reshape(B * S, C_V_WIDTH), row(attn_norm_g)[l], w_gate_b[l], row(b_gate)[l],
            proj_a_b[l], proj_b_b[l], proj_c_b[l], w_out_b[l])
        x = _ffn(x2.reshape(B, S, D), row(ffn_norm_g)[l], w_up_b[l], conv_w[l], row(conv_b)[l],
                 w_down_b[l], final_norm_g.reshape(1, D), final_norm=(l == L - 1))
    return x
```

```python
import functools
import math

import numpy as np
import jax
import jax.numpy as jnp
from jax import lax
from jax.experimental import pallas as pl
from jax.experimental.pallas import tpu as pltpu

D_MODEL = 1024
DEPTH = 4
A_WIDTH = 256
A_GROUPS = 4
A_GDIM = A_WIDTH // A_GROUPS
CHUNK = 128
B_WIDTH = 256
POOL_WINDOWS = (2, 4, 8, 16)
B_GDIM = B_WIDTH // len(POOL_WINDOWS)
C_HEADS = 4
C_HEAD_DIM = 64
C_VDIM = 2 * C_HEAD_DIM
C_QK_WIDTH = C_HEADS * 2 * C_HEAD_DIM
C_V_WIDTH = C_HEADS * C_VDIM
N_BUCKETS = 32
MAX_EXACT = 16
MAX_DISTANCE = 128
D_FF = 2816
EPS = 1e-6
SUBLN_EPS = 1e-5
LN_EPS = 1e-5
MASK_VALUE = -1e30
LOG2_E = math.log2(math.e)

O_A = 2 * A_WIDTH
O_B = O_A + B_WIDTH
O_Q = O_B + C_QK_WIDTH
O_K = O_Q + C_QK_WIDTH
IN_COLS = O_K + C_V_WIDTH

V7X_VMEM_LIMIT_BYTES = 56 * 1024 * 1024
HALO = 16
TM_IN = 512
TM_MERGE = 1024
TM_FFN = 1024
FF_CHUNK = 256
TQ = 1024
TK = 512
QB = 256

F32 = jnp.float32
BF16 = jnp.bfloat16


def _dot(a, b):
    return jnp.dot(a, b, preferred_element_type=F32)


def _gelu(x):
    return 0.5 * x * (1.0 + lax.erf(x * np.float32(math.sqrt(0.5))))


def _rmsnorm(x, g, eps):
    return x * lax.rsqrt(jnp.mean(x * x, axis=-1, keepdims=True) + eps) * g


def _const_spec(shape):
    nd = len(shape)
    return pl.BlockSpec(shape, lambda *_: (0,) * nd, pipeline_mode=pl.Buffered(1))


def _layer_spec(shape, layer):
    nd = len(shape)
    return pl.BlockSpec((None,) + tuple(shape), lambda *_: (layer,) + (0,) * nd,
                        pipeline_mode=pl.Buffered(1))


def _in_proj_kernel(x_ref, g_ref, w_ref, lng_ref, lnb_ref, sw_ref, sb_ref, pw_ref, ps_ref,
                    ya_ref, yb_ref, q_ref, k_ref, vt_ref, zb_scr):
    t = pl.program_id(1)
    tm = x_ref.shape[1]

    @pl.when(t == 0)
    def _():
        zb_scr[0:HALO, :] = jnp.zeros((HALO, B_WIDTH), F32)

    @pl.when(t > 0)
    def _():
        zb_scr[0:HALO, :] = zb_scr[tm:tm + HALO, :]

    h = _rmsnorm(x_ref[0], g_ref[...], EPS).astype(BF16)

    za = _dot(h, w_ref[:, 0:O_A])
    zb = _dot(h, w_ref[:, O_A:O_B])
    k_ref[0] = _dot(h, w_ref[:, O_Q:O_K]).astype(k_ref.dtype)

    za = _gelu(za)
    u = za[:, :A_WIDTH]
    v = za[:, A_WIDTH:]
    mu = jnp.mean(v, axis=-1, keepdims=True)
    vc = v - mu
    var = jnp.mean(vc * vc, axis=-1, keepdims=True)
    vn = (vc * lax.rsqrt(var + LN_EPS) * lng_ref[...] + lnb_ref[...]).astype(BF16)

    vt_ref[0, 0] = _dot(h, w_ref[:, O_K:IN_COLS]).T.astype(vt_ref.dtype)

    row = lax.broadcasted_iota(jnp.int32, (CHUNK, A_GROUPS * CHUNK), 0)
    col = lax.broadcasted_iota(jnp.int32, (CHUNK, A_GROUPS * CHUNK), 1)
    ws = jnp.where((col % CHUNK) <= row, sw_ref[...], 0.0).astype(BF16)
    grp = lax.broadcasted_iota(jnp.int32, (CHUNK, A_WIDTH), 1) // A_GDIM
    for c in range(tm // CHUNK):
        vch = vn[c * CHUNK:(c + 1) * CHUNK, :]
        rhs = jnp.concatenate(
            [jnp.where(grp == g, vch, jnp.zeros_like(vch)) for g in range(A_GROUPS)], axis=0)
        mixed = _dot(ws, rhs) + sb_ref[...]
        ya_ref[0, c * CHUNK:(c + 1) * CHUNK, :] = (
            u[c * CHUNK:(c + 1) * CHUNK, :] * mixed).astype(ya_ref.dtype)

    zq = _dot(h, w_ref[:, O_B:O_Q]) * np.float32(C_HEAD_DIM ** -0.5 * LOG2_E)
    first_half = (lax.broadcasted_iota(jnp.int32, (tm, C_QK_WIDTH), 1) % C_VDIM) < C_HEAD_DIM
    q_ref[0, 0] = jnp.where(first_half, zq, 0.0).astype(q_ref.dtype)
    q_ref[0, 1] = jnp.where(first_half, 0.0, zq).astype(q_ref.dtype)

    zb_scr[HALO:HALO + tm, :] = zb
    pos = t * tm + lax.broadcasted_iota(jnp.int32, (tm, 1), 0)
    lane = lax.broadcasted_iota(jnp.int32, (tm, 128), 1)
    halves = []
    for half in range(2):
        cols = slice(half * 128, (half + 1) * 128)
        w_small, w_big = POOL_WINDOWS[2 * half], POOL_WINDOWS[2 * half + 1]
        acc = zb[:, cols]
        for k in range(1, w_small):
            acc = acc + zb_scr[HALO - k:HALO - k + tm, cols]
        s_small = acc
        for k in range(w_small, w_big):
            acc = acc + zb_scr[HALO - k:HALO - k + tm, cols]
        cnt_small = jnp.minimum(pos + 1, w_small).astype(F32)
        cnt_big = jnp.minimum(pos + 1, w_big).astype(F32)
        pooled = jnp.where(lane < B_GDIM, s_small / cnt_small, acc / cnt_big)
        halves.append(pooled - zb[:, cols])
    p = jnp.concatenate(halves, axis=1).astype(BF16)
    yb_ref[0] = (_dot(p, pw_ref[...]) * ps_ref[...]).astype(yb_ref.dtype)


def _in_proj(layer, x, g, w_in, ln_g, ln_b, sgu_w_cat, sgu_b_tile, pool_w_bd, pool_scale):
    B, S, D = x.shape
    tm = TM_IN
    assert tm == TK, "the transposed value tiles are written one attention key tile at a time"
    grid = (B, S // tm)
    tok = lambda width: pl.BlockSpec((1, tm, width), lambda b, t: (b, t, 0))
    out_shape = (
        jax.ShapeDtypeStruct((B, S, A_WIDTH), BF16),
        jax.ShapeDtypeStruct((B, S, B_WIDTH), BF16),
        jax.ShapeDtypeStruct((B, 2, S, C_QK_WIDTH), BF16),
        jax.ShapeDtypeStruct((B, S, C_QK_WIDTH), BF16),
        jax.ShapeDtypeStruct((B, S // tm, C_V_WIDTH, tm), BF16),
    )
    return pl.pallas_call(
        _in_proj_kernel,
        grid=grid,
        in_specs=[
            tok(D),
            _const_spec((1, D)),
            _layer_spec((D, IN_COLS), layer),
            _const_spec((1, A_WIDTH)),
            _const_spec((1, A_WIDTH)),
            _const_spec((CHUNK, A_GROUPS * CHUNK)),
            _const_spec((CHUNK, A_WIDTH)),
            _const_spec((B_WIDTH, B_WIDTH)),
            _const_spec((1, B_WIDTH)),
        ],
        out_specs=(
            tok(A_WIDTH),
            tok(B_WIDTH),
            pl.BlockSpec((1, 2, tm, C_QK_WIDTH), lambda b, t: (b, 0, t, 0)),
            tok(C_QK_WIDTH),
            pl.BlockSpec((1, 1, C_V_WIDTH, tm), lambda b, t: (b, t, 0, 0)),
        ),
        out_shape=out_shape,
        scratch_shapes=[pltpu.VMEM((HALO + tm, B_WIDTH), F32)],
        compiler_params=pltpu.CompilerParams(
            dimension_semantics=("arbitrary", "arbitrary"),
            vmem_limit_bytes=V7X_VMEM_LIMIT_BYTES),
        name="in_proj",
    )(x, g, w_in, ln_g, ln_b, sgu_w_cat, sgu_b_tile, pool_w_bd, pool_scale)


N_BIAS_BLOCKS = -(-(TK + MAX_DISTANCE - 1) // QB)


def _bucket_tiles():
    r = np.arange(QB)[None, :]
    c = np.arange(TK)[:, None]
    tiles = []
    for dd in range(N_BIAS_BLOCKS):
        rel = dd * QB + r - c
        n = np.maximum(rel, 0)
        nf = np.maximum(n, 1).astype(np.float32)
        large = MAX_EXACT + (np.log(nf / MAX_EXACT) / math.log(MAX_DISTANCE / MAX_EXACT)
                             * (N_BUCKETS - MAX_EXACT)).astype(np.int32)
        large = np.minimum(large, N_BUCKETS - 1)
        bucket = np.where(n < MAX_EXACT, n, large)
        tiles.append(np.where(rel >= 0, bucket, -1))
    return np.stack(tiles).astype(np.int32)


def _bias_kernel(rb_ref, bk_ref, o_ref):
    for h in range(C_HEADS):
        far = rb_ref[N_BUCKETS - 1, h]
        for d in range(N_BIAS_BLOCKS):
            bk = bk_ref[d]
            acc = jnp.zeros(bk.shape, F32)
            for b in range(N_BUCKETS - 1):
                acc = jnp.where(bk == b, (rb_ref[b, h] - far) * np.float32(LOG2_E), acc)
            o_ref[h, d] = jnp.where(bk < 0, np.float32(MASK_VALUE), acc)


def _bias_tiles(rel_bias):
    buckets = jnp.asarray(_bucket_tiles())
    return pl.pallas_call(
        _bias_kernel,
        in_specs=[pl.BlockSpec(memory_space=pltpu.SMEM),
                  pl.BlockSpec(memory_space=pltpu.VMEM)],
        out_specs=pl.BlockSpec(memory_space=pltpu.VMEM),
        out_shape=jax.ShapeDtypeStruct((C_HEADS, N_BIAS_BLOCKS, TK, QB), F32),
        name="bias_tiles",
    )(rel_bias, buckets)


def _visible_keys(delta):
    return min(TK, max(0, delta + QB))


def _first_biased_key(delta):
    return min(TK, max(0, (delta - MAX_DISTANCE + 1) // 8 * 8))


def _attn_kernel(sc_ref, q_ref, k_ref, vt_ref, bias_ref, lamv_ref, g_ref, o_ref,
                 m_scr, l_scr, acc_scr):
    qi = pl.program_id(1)
    lam_init = sc_ref[0]
    lv = lamv_ref[...]
    lam = (jnp.exp(jnp.sum(lv[0:1] * lv[1:2], axis=-1, keepdims=True))
           - jnp.exp(jnp.sum(lv[2:3] * lv[3:4], axis=-1, keepdims=True)) + lam_init)

    m_scr[...] = jnp.full(m_scr.shape, -jnp.inf, F32)
    l_scr[...] = jnp.zeros(l_scr.shape, F32)
    acc_scr[...] = jnp.zeros(acc_scr.shape, F32)

    def step(j, koff):
        start = pl.multiple_of(j * TK, TK)
        chains = []
        for h in range(C_HEADS):
            for c in range(2):
                for b in range(TQ // QB):
                    delta = None if koff is None else b * QB - koff
                    nk = TK if koff is None else _visible_keys(delta)
                    if nk > 0:
                        chains.append((h, c, b, delta, nk))
        scores = {}
        for h, c, b, delta, nk in chains:
            cols = slice(h * C_VDIM, (h + 1) * C_VDIM)
            kt = k_ref[0, pl.ds(start, nk), cols]
            qb = q_ref[0, c, b * QB:(b + 1) * QB, cols]
            s = lax.dot_general(
                kt, qb, (((1,), (1,)), ((), ())), preferred_element_type=F32)
            lo = TK if koff is None else _first_biased_key(delta)
            if lo < nk:
                biased = s[lo:] + bias_ref[h, delta // QB, lo:nk, :]
                s = biased if lo == 0 else jnp.concatenate([s[:lo], biased], axis=0)
            scores[h, c, b] = s
        for h, c, b, delta, nk in chains:
            cols = slice(h * C_VDIM, (h + 1) * C_VDIM)
            qsl = slice(c * TQ + b * QB, c * TQ + (b + 1) * QB)
            vt = vt_ref[0, j, cols, 0:nk]
            s = scores[h, c, b]
            m_prev = m_scr[h, :, qsl]
            m_new = jnp.maximum(m_prev, jnp.max(s, axis=0, keepdims=True))
            alpha = jnp.exp2(m_prev - m_new)
            p = jnp.exp2(s - m_new)
            l_scr[h, :, qsl] = alpha * l_scr[h, :, qsl] + jnp.sum(p, axis=0, keepdims=True)
            acc_scr[h, :, qsl] = alpha * acc_scr[h, :, qsl] + _dot(vt, p.astype(BF16))
            m_scr[h, :, qsl] = m_new

    def plain_step(j, carry):
        step(j, None)
        return carry

    tiles_per_q = TQ // TK
    first_near = tiles_per_q * qi - 1
    lax.fori_loop(0, jnp.maximum(first_near, 0), plain_step, 0)

    @pl.when(qi >= 1)
    def _():
        step(first_near, -TK)

    for t in range(tiles_per_q):
        step(tiles_per_q * qi + t, t * TK)

    for h in range(C_HEADS):
        cols = slice(h * C_VDIM, (h + 1) * C_VDIM)
        o_all = acc_scr[h] / l_scr[h]
        o = o_all[:, :TQ] - lam * o_all[:, TQ:]
        o = o * lax.rsqrt(jnp.mean(o * o, axis=0, keepdims=True) + SUBLN_EPS)
        o = o * g_ref[...] * (1.0 - lam_init)
        o_ref[0, :, cols] = o.T.astype(o_ref.dtype)


def _attention(scalars, q, k, vt, bias, lam_vecs, subln_g):
    B, S, _ = k.shape
    grid = (B, S // TQ)
    return pl.pallas_call(
        _attn_kernel,
        grid=grid,
        in_specs=[
            pl.BlockSpec(memory_space=pltpu.SMEM),
            pl.BlockSpec((1, 2, TQ, C_QK_WIDTH), lambda b, i: (b, 0, i, 0)),
            pl.BlockSpec((1, S, C_QK_WIDTH), lambda b, i: (b, 0, 0),
                         pipeline_mode=pl.Buffered(1)),
            pl.BlockSpec((1, S // TK, C_V_WIDTH, TK), lambda b, i: (b, 0, 0, 0),
                         pipeline_mode=pl.Buffered(1)),
            _const_spec((C_HEADS, N_BIAS_BLOCKS, TK, QB)),
            _const_spec((4, C_HEAD_DIM)),
            _const_spec((C_VDIM, 1)),
        ],
        out_specs=pl.BlockSpec((1, TQ, C_V_WIDTH), lambda b, i: (b, i, 0)),
        out_shape=jax.ShapeDtypeStruct((B, S, C_V_WIDTH), BF16),
        scratch_shapes=[
            pltpu.VMEM((C_HEADS, 1, 2 * TQ), F32),
            pltpu.VMEM((C_HEADS, 1, 2 * TQ), F32),
            pltpu.VMEM((C_HEADS, C_VDIM, 2 * TQ), F32),
        ],
        compiler_params=pltpu.CompilerParams(
            dimension_semantics=("arbitrary", "arbitrary"),
            vmem_limit_bytes=V7X_VMEM_LIMIT_BYTES),
        name="diff_attn",
    )(scalars, q, k, vt, bias, lam_vecs, subln_g)


def _merge_kernel(x_ref, ya_ref, yb_ref, yc_ref, g_ref, wg_ref, bg_ref,
                  pa_ref, pb_ref, pc_ref, wo_ref, o_ref):
    x = x_ref[...]
    h = _rmsnorm(x, g_ref[...], EPS).astype(BF16)
    merged = None
    for i, (y_ref, p_ref) in enumerate(((ya_ref, pa_ref), (yb_ref, pb_ref), (yc_ref, pc_ref))):
        cols = slice(i * D_MODEL, (i + 1) * D_MODEL)
        pre = _dot(h, wg_ref[:, cols]) + bg_ref[:, cols]
        gate = 1.0 / (1.0 + jnp.exp(-pre))
        term = gate * _dot(y_ref[...], p_ref[...])
        merged = term if merged is None else merged + term
    o_ref[...] = x + _dot(merged.astype(BF16), wo_ref[...])


def _merge(layer, x2, ya, yb, yc, g, w_gate, b_gate, proj_a, proj_b, proj_c, w_out):
    N, D = x2.shape
    tm = TM_MERGE
    tok = lambda width: pl.BlockSpec((tm, width), lambda t: (t, 0))
    return pl.pallas_call(
        _merge_kernel,
        grid=(N // tm,),
        in_specs=[
            tok(D), tok(A_WIDTH), tok(B_WIDTH), tok(C_V_WIDTH),
            _const_spec((1, D)),
            _layer_spec((D, 3 * D), layer),
            _const_spec((1, 3 * D)),
            _layer_spec((A_WIDTH, D), layer),
            _layer_spec((B_WIDTH, D), layer),
            _layer_spec((C_V_WIDTH, D), layer),
            _layer_spec((D, D), layer),
        ],
        out_specs=tok(D),
        out_shape=jax.ShapeDtypeStruct((N, D), F32),
        compiler_params=pltpu.CompilerParams(
            dimension_semantics=("arbitrary",),
            vmem_limit_bytes=V7X_VMEM_LIMIT_BYTES),
        name="merge",
    )(x2, ya, yb, yc, g, w_gate, b_gate, proj_a, proj_b, proj_c, w_out)


def _ffn_kernel(x_ref, g_ref, wu_ref, cw_ref, cb_ref, wd_ref, fg_ref, o_ref, halo_scr, act_scr,
                *, final_norm):
    t = pl.program_id(1)
    tm = x_ref.shape[1]
    x = x_ref[0]
    h = _rmsnorm(x, g_ref[...], EPS).astype(BF16)
    row = lax.broadcasted_iota(jnp.int32, (tm, FF_CHUNK), 0)

    @pl.when(t == 0)
    def _():
        halo_scr[...] = jnp.zeros(halo_scr.shape, F32)

    for c in range(D_FF // FF_CHUNK):
        cols = slice(c * FF_CHUNK, (c + 1) * FF_CHUNK)
        gate = _dot(h, wu_ref[:, cols])
        up = _dot(h, wu_ref[:, D_FF + c * FF_CHUNK:D_FF + (c + 1) * FF_CHUNK])
        prev1 = halo_scr[7:8, cols]
        prev2 = halo_scr[6:7, cols]
        g1 = jnp.where(row == 0, prev1, pltpu.roll(gate, 1, 0))
        g2 = jnp.where(row == 0, prev2, jnp.where(row == 1, prev1, pltpu.roll(gate, 2, 0)))
        halo_scr[:, cols] = gate[tm - 8:tm, :]
        conv = (g2 * cw_ref[0:1, cols] + g1 * cw_ref[1:2, cols] + gate * cw_ref[2:3, cols]
                + cb_ref[:, cols])
        act_scr[:, cols] = (_gelu(conv) * up).astype(BF16)
    out = x + _dot(act_scr[...], wd_ref[...])
    if final_norm:
        out = _rmsnorm(out, fg_ref[...], EPS)
    o_ref[0] = out


def _ffn(layer, x, g, w_up, conv_w, conv_b, w_down, final_g, final_norm):
    B, S, D = x.shape
    tm = TM_FFN
    tok = pl.BlockSpec((1, tm, D), lambda b, t: (b, t, 0))
    return pl.pallas_call(
        functools.partial(_ffn_kernel, final_norm=final_norm),
        grid=(B, S // tm),
        in_specs=[
            tok,
            _const_spec((1, D)),
            _layer_spec((D, 2 * D_FF), layer),
            _const_spec((3, D_FF)),
            _const_spec((1, D_FF)),
            _layer_spec((D_FF, D), layer),
            _const_spec((1, D)),
        ],
        out_specs=tok,
        out_shape=jax.ShapeDtypeStruct((B, S, D), F32),
        scratch_shapes=[pltpu.VMEM((8, D_FF), F32), pltpu.VMEM((tm, D_FF), BF16)],
        compiler_params=pltpu.CompilerParams(
            dimension_semantics=("arbitrary", "arbitrary"),
            vmem_limit_bytes=V7X_VMEM_LIMIT_BYTES),
        name="ffn_final" if final_norm else "ffn",
    )(x, g, w_up, conv_w, conv_b, w_down, final_g)


def kernel(x, attn_norm_g, w_in, w_gate, b_gate, sgu_ln_g, sgu_ln_b, sgu_w, sgu_b, proj_a, pool_w,
           pool_scale, proj_b, diff_lam, diff_subln_g, proj_c, w_out, ffn_norm_g, w_up, conv_w,
           conv_b, w_down, rel_bias, final_norm_g):
    B, S, D = x.shape
    L = DEPTH
    row = lambda a: a.reshape(a.shape[0], 1, a.shape[1])

    w_in_b = w_in.astype(BF16)
    w_gate_b = w_gate.astype(BF16)
    proj_a_b = proj_a.astype(BF16)
    proj_b_b = proj_b.astype(BF16)
    proj_c_b = proj_c.astype(BF16)
    w_out_b = w_out.astype(BF16)
    w_up_b = w_up.astype(BF16)
    w_down_b = w_down.astype(BF16)
    sgu_w_cat = jnp.transpose(sgu_w, (0, 2, 1, 3)).reshape(L, CHUNK, A_GROUPS * CHUNK)
    sgu_b_tile = jnp.repeat(jnp.transpose(sgu_b, (0, 2, 1)), A_GDIM, axis=2)
    eye = jnp.eye(len(POOL_WINDOWS), dtype=pool_w.dtype)
    pool_w_bd = jnp.einsum('lgcd,gh->lgchd', pool_w, eye).reshape(L, B_WIDTH, B_WIDTH).astype(BF16)

    bias = _bias_tiles(rel_bias)

    for l in range(L):
        lam_init = 0.8 - 0.6 * math.exp(-0.3 * l)
        scalars = jnp.asarray([lam_init], F32)
        ya, yb, q, k, vt = _in_proj(
            l, x, row(attn_norm_g)[l], w_in_b, row(sgu_ln_g)[l], row(sgu_ln_b)[l],
            sgu_w_cat[l], sgu_b_tile[l], pool_w_bd[l], row(pool_scale)[l])
        yc = _attention(scalars, q, k, vt, bias, diff_lam[l], diff_subln_g[l].reshape(C_VDIM, 1))
        x2 = _merge(
            l, x.reshape(B * S, D), ya.reshape(B * S, A_WIDTH), yb.reshape(B * S, B_WIDTH),
            yc.reshape(B * S, C_V_WIDTH), row(attn_norm_g)[l], w_gate_b, row(b_gate)[l],
            proj_a_b, proj_b_b, proj_c_b, w_out_b)
        x = _ffn(l, x2.reshape(B, S, D), row(ffn_norm_g)[l], w_up_b, conv_w[l], row(conv_b)[l],
                 w_down_b, final_norm_g.reshape(1, D), final_norm=(l == L - 1))
    return x
```

```python
import functools
import math

import numpy as np
import jax
import jax.numpy as jnp
from jax import lax
from jax.experimental import pallas as pl
from jax.experimental.pallas import tpu as pltpu

D_MODEL = 1024
DEPTH = 4
A_WIDTH = 256
A_GROUPS = 4
A_GDIM = A_WIDTH // A_GROUPS
CHUNK = 128
B_WIDTH = 256
POOL_WINDOWS = (2, 4, 8, 16)
B_GDIM = B_WIDTH // len(POOL_WINDOWS)
C_HEADS = 4
C_HEAD_DIM = 64
C_VDIM = 2 * C_HEAD_DIM
C_QK_WIDTH = C_HEADS * 2 * C_HEAD_DIM
C_V_WIDTH = C_HEADS * C_VDIM
N_BUCKETS = 32
MAX_EXACT = 16
MAX_DISTANCE = 128
D_FF = 2816
EPS = 1e-6
SUBLN_EPS = 1e-5
LN_EPS = 1e-5
MASK_VALUE = -1e30
LOG2_E = math.log2(math.e)

O_A = 2 * A_WIDTH
O_B = O_A + B_WIDTH
O_Q = O_B + C_QK_WIDTH
O_K = O_Q + C_QK_WIDTH
IN_COLS = O_K + C_V_WIDTH

V7X_VMEM_LIMIT_BYTES = 56 * 1024 * 1024
HALO = 16
TM_IN = 512
TM_MERGE = 1024
TM_FFN = 1024
FF_CHUNK = 256
TQ = 1024
TK = 512
QB = 256
SCORE_LOOKAHEAD = 4

F32 = jnp.float32
BF16 = jnp.bfloat16


def _dot(a, b):
    return jnp.dot(a, b, preferred_element_type=F32)


def _gelu(x):
    return 0.5 * x * (1.0 + lax.erf(x * np.float32(math.sqrt(0.5))))


def _rmsnorm(x, g, eps):
    return x * lax.rsqrt(jnp.mean(x * x, axis=-1, keepdims=True) + eps) * g


def _const_spec(shape):
    nd = len(shape)
    return pl.BlockSpec(shape, lambda *_: (0,) * nd, pipeline_mode=pl.Buffered(1))


def _layer_spec(shape, layer):
    nd = len(shape)
    return pl.BlockSpec((None,) + tuple(shape), lambda *_: (layer,) + (0,) * nd,
                        pipeline_mode=pl.Buffered(1))


def _in_proj_kernel(x_ref, g_ref, w_ref, lng_ref, lnb_ref, sw_ref, sb_ref, pw_ref, ps_ref,
                    ya_ref, yb_ref, q_ref, k_ref, vt_ref, zb_scr):
    t = pl.program_id(1)
    tm = x_ref.shape[1]

    @pl.when(t == 0)
    def _():
        zb_scr[0:HALO, :] = jnp.zeros((HALO, B_WIDTH), F32)

    @pl.when(t > 0)
    def _():
        zb_scr[0:HALO, :] = zb_scr[tm:tm + HALO, :]

    h = _rmsnorm(x_ref[0], g_ref[...], EPS).astype(BF16)

    za = _dot(h, w_ref[:, 0:O_A])
    zb = _dot(h, w_ref[:, O_A:O_B])
    k_ref[0] = _dot(h, w_ref[:, O_Q:O_K]).astype(k_ref.dtype)

    za = _gelu(za)
    u = za[:, :A_WIDTH]
    v = za[:, A_WIDTH:]
    mu = jnp.mean(v, axis=-1, keepdims=True)
    vc = v - mu
    var = jnp.mean(vc * vc, axis=-1, keepdims=True)
    vn = (vc * lax.rsqrt(var + LN_EPS) * lng_ref[...] + lnb_ref[...]).astype(BF16)

    vt_ref[0, 0] = _dot(h, w_ref[:, O_K:IN_COLS]).T.astype(vt_ref.dtype)

    row = lax.broadcasted_iota(jnp.int32, (CHUNK, A_GROUPS * CHUNK), 0)
    col = lax.broadcasted_iota(jnp.int32, (CHUNK, A_GROUPS * CHUNK), 1)
    ws = jnp.where((col % CHUNK) <= row, sw_ref[...], 0.0).astype(BF16)
    grp = lax.broadcasted_iota(jnp.int32, (CHUNK, A_WIDTH), 1) // A_GDIM
    for c in range(tm // CHUNK):
        vch = vn[c * CHUNK:(c + 1) * CHUNK, :]
        rhs = jnp.concatenate(
            [jnp.where(grp == g, vch, jnp.zeros_like(vch)) for g in range(A_GROUPS)], axis=0)
        mixed = _dot(ws, rhs) + sb_ref[...]
        ya_ref[0, c * CHUNK:(c + 1) * CHUNK, :] = (
            u[c * CHUNK:(c + 1) * CHUNK, :] * mixed).astype(ya_ref.dtype)

    zq = _dot(h, w_ref[:, O_B:O_Q]) * np.float32(C_HEAD_DIM ** -0.5 * LOG2_E)
    first_half = (lax.broadcasted_iota(jnp.int32, (tm, C_QK_WIDTH), 1) % C_VDIM) < C_HEAD_DIM
    q_ref[0, 0] = jnp.where(first_half, zq, 0.0).astype(q_ref.dtype)
    q_ref[0, 1] = jnp.where(first_half, 0.0, zq).astype(q_ref.dtype)

    zb_scr[HALO:HALO + tm, :] = zb
    pos = t * tm + lax.broadcasted_iota(jnp.int32, (tm, 1), 0)
    lane = lax.broadcasted_iota(jnp.int32, (tm, 128), 1)
    halves = []
    for half in range(2):
        cols = slice(half * 128, (half + 1) * 128)
        w_small, w_big = POOL_WINDOWS[2 * half], POOL_WINDOWS[2 * half + 1]
        acc = zb[:, cols]
        for k in range(1, w_small):
            acc = acc + zb_scr[HALO - k:HALO - k + tm, cols]
        s_small = acc
        for k in range(w_small, w_big):
            acc = acc + zb_scr[HALO - k:HALO - k + tm, cols]
        cnt_small = jnp.minimum(pos + 1, w_small).astype(F32)
        cnt_big = jnp.minimum(pos + 1, w_big).astype(F32)
        pooled = jnp.where(lane < B_GDIM, s_small / cnt_small, acc / cnt_big)
        halves.append(pooled - zb[:, cols])
    p = jnp.concatenate(halves, axis=1).astype(BF16)
    yb_ref[0] = (_dot(p, pw_ref[...]) * ps_ref[...]).astype(yb_ref.dtype)


def _in_proj(layer, x, g, w_in, ln_g, ln_b, sgu_w_cat, sgu_b_tile, pool_w_bd, pool_scale):
    B, S, D = x.shape
    tm = TM_IN
    assert tm == TK, "the transposed value tiles are written one attention key tile at a time"
    grid = (B, S // tm)
    tok = lambda width: pl.BlockSpec((1, tm, width), lambda b, t: (b, t, 0))
    out_shape = (
        jax.ShapeDtypeStruct((B, S, A_WIDTH), BF16),
        jax.ShapeDtypeStruct((B, S, B_WIDTH), BF16),
        jax.ShapeDtypeStruct((B, 2, S, C_QK_WIDTH), BF16),
        jax.ShapeDtypeStruct((B, S, C_QK_WIDTH), BF16),
        jax.ShapeDtypeStruct((B, S // tm, C_V_WIDTH, tm), BF16),
    )
    return pl.pallas_call(
        _in_proj_kernel,
        grid=grid,
        in_specs=[
            tok(D),
            _const_spec((1, D)),
            _layer_spec((D, IN_COLS), layer),
            _const_spec((1, A_WIDTH)),
            _const_spec((1, A_WIDTH)),
            _const_spec((CHUNK, A_GROUPS * CHUNK)),
            _const_spec((CHUNK, A_WIDTH)),
            _const_spec((B_WIDTH, B_WIDTH)),
            _const_spec((1, B_WIDTH)),
        ],
        out_specs=(
            tok(A_WIDTH),
            tok(B_WIDTH),
            pl.BlockSpec((1, 2, tm, C_QK_WIDTH), lambda b, t: (b, 0, t, 0)),
            tok(C_QK_WIDTH),
            pl.BlockSpec((1, 1, C_V_WIDTH, tm), lambda b, t: (b, t, 0, 0)),
        ),
        out_shape=out_shape,
        scratch_shapes=[pltpu.VMEM((HALO + tm, B_WIDTH), F32)],
        compiler_params=pltpu.CompilerParams(
            dimension_semantics=("arbitrary", "arbitrary"),
            vmem_limit_bytes=V7X_VMEM_LIMIT_BYTES),
        name="in_proj",
    )(x, g, w_in, ln_g, ln_b, sgu_w_cat, sgu_b_tile, pool_w_bd, pool_scale)


N_BIAS_BLOCKS = -(-(TK + MAX_DISTANCE - 1) // QB)


def _bucket_tiles():
    r = np.arange(QB)[None, :]
    c = np.arange(TK)[:, None]
    tiles = []
    for dd in range(N_BIAS_BLOCKS):
        rel = dd * QB + r - c
        n = np.maximum(rel, 0)
        nf = np.maximum(n, 1).astype(np.float32)
        large = MAX_EXACT + (np.log(nf / MAX_EXACT) / math.log(MAX_DISTANCE / MAX_EXACT)
                             * (N_BUCKETS - MAX_EXACT)).astype(np.int32)
        large = np.minimum(large, N_BUCKETS - 1)
        bucket = np.where(n < MAX_EXACT, n, large)
        tiles.append(np.where(rel >= 0, bucket, -1))
    return np.stack(tiles).astype(np.int32)


def _bias_kernel(rb_ref, bk_ref, o_ref):
    for h in range(C_HEADS):
        far = rb_ref[N_BUCKETS - 1, h]
        for d in range(N_BIAS_BLOCKS):
            bk = bk_ref[d]
            acc = jnp.zeros(bk.shape, F32)
            for b in range(N_BUCKETS - 1):
                acc = jnp.where(bk == b, (rb_ref[b, h] - far) * np.float32(LOG2_E), acc)
            o_ref[h, d] = jnp.where(bk < 0, np.float32(MASK_VALUE), acc)


def _bias_tiles(rel_bias):
    buckets = jnp.asarray(_bucket_tiles())
    return pl.pallas_call(
        _bias_kernel,
        in_specs=[pl.BlockSpec(memory_space=pltpu.SMEM),
                  pl.BlockSpec(memory_space=pltpu.VMEM)],
        out_specs=pl.BlockSpec(memory_space=pltpu.VMEM),
        out_shape=jax.ShapeDtypeStruct((C_HEADS, N_BIAS_BLOCKS, TK, QB), F32),
        name="bias_tiles",
    )(rel_bias, buckets)


def _visible_keys(delta):
    return min(TK, max(0, delta + QB))


def _first_biased_key(delta):
    return min(TK, max(0, (delta - MAX_DISTANCE + 1) // 8 * 8))


def _attn_kernel(sc_ref, q_ref, k_ref, vt_ref, bias_ref, lamv_ref, g_ref, o_ref,
                 m_scr, l_scr, acc_scr):
    qi = pl.program_id(1)
    lam_init = sc_ref[0]
    lv = lamv_ref[...]
    lam = (jnp.exp(jnp.sum(lv[0:1] * lv[1:2], axis=-1, keepdims=True))
           - jnp.exp(jnp.sum(lv[2:3] * lv[3:4], axis=-1, keepdims=True)) + lam_init)

    m_scr[...] = jnp.full(m_scr.shape, -jnp.inf, F32)
    l_scr[...] = jnp.zeros(l_scr.shape, F32)
    acc_scr[...] = jnp.zeros(acc_scr.shape, F32)

    def step(j, koff):
        start = pl.multiple_of(j * TK, TK)
        chains = []
        for h in range(C_HEADS):
            for c in range(2):
                for b in range(TQ // QB):
                    delta = None if koff is None else b * QB - koff
                    nk = TK if koff is None else _visible_keys(delta)
                    if nk > 0:
                        chains.append((h, c, b, delta, nk))
        def score(chain):
            h, c, b, delta, nk = chain
            cols = slice(h * C_VDIM, (h + 1) * C_VDIM)
            kt = k_ref[0, pl.ds(start, nk), cols]
            qb = q_ref[0, c, b * QB:(b + 1) * QB, cols]
            s = lax.dot_general(
                kt, qb, (((1,), (1,)), ((), ())), preferred_element_type=F32)
            lo = TK if koff is None else _first_biased_key(delta)
            if lo < nk:
                biased = s[lo:] + bias_ref[h, delta // QB, lo:nk, :]
                s = biased if lo == 0 else jnp.concatenate([s[:lo], biased], axis=0)
            return s

        def consume(chain, s):
            h, c, b, delta, nk = chain
            cols = slice(h * C_VDIM, (h + 1) * C_VDIM)
            qsl = slice(c * TQ + b * QB, c * TQ + (b + 1) * QB)
            vt = vt_ref[0, j, cols, 0:nk]
            m_prev = m_scr[h, :, qsl]
            m_new = jnp.maximum(m_prev, jnp.max(s, axis=0, keepdims=True))
            alpha = jnp.exp2(m_prev - m_new)
            p = jnp.exp2(s - m_new)
            l_scr[h, :, qsl] = alpha * l_scr[h, :, qsl] + jnp.sum(p, axis=0, keepdims=True)
            acc_scr[h, :, qsl] = alpha * acc_scr[h, :, qsl] + _dot(vt, p.astype(BF16))
            m_scr[h, :, qsl] = m_new

        scores = [score(ch) for ch in chains[:SCORE_LOOKAHEAD]]
        for i, ch in enumerate(chains):
            if i + SCORE_LOOKAHEAD < len(chains):
                scores.append(score(chains[i + SCORE_LOOKAHEAD]))
            consume(ch, scores[i])
            scores[i] = None

    def plain_step(j, carry):
        step(j, None)
        return carry

    tiles_per_q = TQ // TK
    first_near = tiles_per_q * qi - 1
    lax.fori_loop(0, jnp.maximum(first_near, 0), plain_step, 0)

    @pl.when(qi >= 1)
    def _():
        step(first_near, -TK)

    for t in range(tiles_per_q):
        step(tiles_per_q * qi + t, t * TK)

    for h in range(C_HEADS):
        cols = slice(h * C_VDIM, (h + 1) * C_VDIM)
        o_all = acc_scr[h] / l_scr[h]
        o = o_all[:, :TQ] - lam * o_all[:, TQ:]
        o = o * lax.rsqrt(jnp.mean(o * o, axis=0, keepdims=True) + SUBLN_EPS)
        o = o * g_ref[...] * (1.0 - lam_init)
        o_ref[0, :, cols] = o.T.astype(o_ref.dtype)


def _attention(scalars, q, k, vt, bias, lam_vecs, subln_g):
    B, S, _ = k.shape
    grid = (B, S // TQ)
    return pl.pallas_call(
        _attn_kernel,
        grid=grid,
        in_specs=[
            pl.BlockSpec(memory_space=pltpu.SMEM),
            pl.BlockSpec((1, 2, TQ, C_QK_WIDTH), lambda b, i: (b, 0, i, 0)),
            pl.BlockSpec((1, S, C_QK_WIDTH), lambda b, i: (b, 0, 0)),
            pl.BlockSpec((1, S // TK, C_V_WIDTH, TK), lambda b, i: (b, 0, 0, 0)),
            _const_spec((C_HEADS, N_BIAS_BLOCKS, TK, QB)),
            _const_spec((4, C_HEAD_DIM)),
            _const_spec((C_VDIM, 1)),
        ],
        out_specs=pl.BlockSpec((1, TQ, C_V_WIDTH), lambda b, i: (b, i, 0)),
        out_shape=jax.ShapeDtypeStruct((B, S, C_V_WIDTH), BF16),
        scratch_shapes=[
            pltpu.VMEM((C_HEADS, 1, 2 * TQ), F32),
            pltpu.VMEM((C_HEADS, 1, 2 * TQ), F32),
            pltpu.VMEM((C_HEADS, C_VDIM, 2 * TQ), F32),
        ],
        compiler_params=pltpu.CompilerParams(
            dimension_semantics=("arbitrary", "arbitrary"),
            vmem_limit_bytes=V7X_VMEM_LIMIT_BYTES),
        name="diff_attn",
    )(scalars, q, k, vt, bias, lam_vecs, subln_g)


def _merge_kernel(x_ref, ya_ref, yb_ref, yc_ref, g_ref, wg_ref, bg_ref,
                  pa_ref, pb_ref, pc_ref, wo_ref, o_ref):
    x = x_ref[...]
    h = _rmsnorm(x, g_ref[...], EPS).astype(BF16)
    merged = None
    for i, (y_ref, p_ref) in enumerate(((ya_ref, pa_ref), (yb_ref, pb_ref), (yc_ref, pc_ref))):
        cols = slice(i * D_MODEL, (i + 1) * D_MODEL)
        pre = _dot(h, wg_ref[:, cols]) + bg_ref[:, cols]
        gate = 1.0 / (1.0 + jnp.exp(-pre))
        term = gate * _dot(y_ref[...], p_ref[...])
        merged = term if merged is None else merged + term
    o_ref[...] = x + _dot(merged.astype(BF16), wo_ref[...])


def _merge(layer, x2, ya, yb, yc, g, w_gate, b_gate, proj_a, proj_b, proj_c, w_out):
    N, D = x2.shape
    tm = TM_MERGE
    tok = lambda width: pl.BlockSpec((tm, width), lambda t: (t, 0))
    return pl.pallas_call(
        _merge_kernel,
        grid=(N // tm,),
        in_specs=[
            tok(D), tok(A_WIDTH), tok(B_WIDTH), tok(C_V_WIDTH),
            _const_spec((1, D)),
            _layer_spec((D, 3 * D), layer),
            _const_spec((1, 3 * D)),
            _layer_spec((A_WIDTH, D), layer),
            _layer_spec((B_WIDTH, D), layer),
            _layer_spec((C_V_WIDTH, D), layer),
            _layer_spec((D, D), layer),
        ],
        out_specs=tok(D),
        out_shape=jax.ShapeDtypeStruct((N, D), F32),
        compiler_params=pltpu.CompilerParams(
            dimension_semantics=("arbitrary",),
            vmem_limit_bytes=V7X_VMEM_LIMIT_BYTES),
        name="merge",
    )(x2, ya, yb, yc, g, w_gate, b_gate, proj_a, proj_b, proj_c, w_out)


def _ffn_kernel(x_ref, g_ref, wu_ref, cw_ref, cb_ref, wd_ref, fg_ref, o_ref, halo_scr, act_scr,
                *, final_norm):
    t = pl.program_id(1)
    tm = x_ref.shape[1]
    x = x_ref[0]
    h = _rmsnorm(x, g_ref[...], EPS).astype(BF16)
    row = lax.broadcasted_iota(jnp.int32, (tm, FF_CHUNK), 0)

    @pl.when(t == 0)
    def _():
        halo_scr[...] = jnp.zeros(halo_scr.shape, F32)

    for c in range(D_FF // FF_CHUNK):
        cols = slice(c * FF_CHUNK, (c + 1) * FF_CHUNK)
        gate = _dot(h, wu_ref[:, cols])
        up = _dot(h, wu_ref[:, D_FF + c * FF_CHUNK:D_FF + (c + 1) * FF_CHUNK])
        prev1 = halo_scr[7:8, cols]
        prev2 = halo_scr[6:7, cols]
        g1 = jnp.where(row == 0, prev1, pltpu.roll(gate, 1, 0))
        g2 = jnp.where(row == 0, prev2, jnp.where(row == 1, prev1, pltpu.roll(gate, 2, 0)))
        halo_scr[:, cols] = gate[tm - 8:tm, :]
        conv = (g2 * cw_ref[0:1, cols] + g1 * cw_ref[1:2, cols] + gate * cw_ref[2:3, cols]
                + cb_ref[:, cols])
        act_scr[:, cols] = (_gelu(conv) * up).astype(BF16)
    out = x + _dot(act_scr[...], wd_ref[...])
    if final_norm:
        out = _rmsnorm(out, fg_ref[...], EPS)
    o_ref[0] = out


def _ffn(layer, x, g, w_up, conv_w, conv_b, w_down, final_g, final_norm):
    B, S, D = x.shape
    tm = TM_FFN
    tok = pl.BlockSpec((1, tm, D), lambda b, t: (b, t, 0))
    return pl.pallas_call(
        functools.partial(_ffn_kernel, final_norm=final_norm),
        grid=(B, S // tm),
        in_specs=[
            tok,
            _const_spec((1, D)),
            _layer_spec((D, 2 * D_FF), layer),
            _const_spec((3, D_FF)),
            _const_spec((1, D_FF)),
            _layer_spec((D_FF, D), layer),
            _const_spec((1, D)),
        ],
        out_specs=tok,
        out_shape=jax.ShapeDtypeStruct((B, S, D), F32),
        scratch_shapes=[pltpu.VMEM((8, D_FF), F32), pltpu.VMEM((tm, D_FF), BF16)],
        compiler_params=pltpu.CompilerParams(
            dimension_semantics=("arbitrary", "arbitrary"),
            vmem_limit_bytes=V7X_VMEM_LIMIT_BYTES),
        name="ffn_final" if final_norm else "ffn",
    )(x, g, w_up, conv_w, conv_b, w_down, final_g)


def kernel(x, attn_norm_g, w_in, w_gate, b_gate, sgu_ln_g, sgu_ln_b, sgu_w, sgu_b, proj_a, pool_w,
           pool_scale, proj_b, diff_lam, diff_subln_g, proj_c, w_out, ffn_norm_g, w_up, conv_w,
           conv_b, w_down, rel_bias, final_norm_g):
    B, S, D = x.shape
    L = DEPTH
    row = lambda a: a.reshape(a.shape[0], 1, a.shape[1])

    w_in_b = w_in.astype(BF16)
    w_gate_b = w_gate.astype(BF16)
    proj_a_b = proj_a.astype(BF16)
    proj_b_b = proj_b.astype(BF16)
    proj_c_b = proj_c.astype(BF16)
    w_out_b = w_out.astype(BF16)
    w_up_b = w_up.astype(BF16)
    w_down_b = w_down.astype(BF16)
    sgu_w_cat = jnp.transpose(sgu_w, (0, 2, 1, 3)).reshape(L, CHUNK, A_GROUPS * CHUNK)
    sgu_b_tile = jnp.repeat(jnp.transpose(sgu_b, (0, 2, 1)), A_GDIM, axis=2)
    eye = jnp.eye(len(POOL_WINDOWS), dtype=pool_w.dtype)
    pool_w_bd = jnp.einsum('lgcd,gh->lgchd', pool_w, eye).reshape(L, B_WIDTH, B_WIDTH).astype(BF16)

    bias = _bias_tiles(rel_bias)

    for l in range(L):
        lam_init = 0.8 - 0.6 * math.exp(-0.3 * l)
        scalars = jnp.asarray([lam_init], F32)
        ya, yb, q, k, vt = _in_proj(
            l, x, row(attn_norm_g)[l], w_in_b, row(sgu_ln_g)[l], row(sgu_ln_b)[l],
            sgu_w_cat[l], sgu_b_tile[l], pool_w_bd[l], row(pool_scale)[l])
        yc = _attention(scalars, q, k, vt, bias, diff_lam[l], diff_subln_g[l].reshape(C_VDIM, 1))
        x2 = _merge(
            l, x.reshape(B * S, D), ya.reshape(B * S, A_WIDTH), yb.reshape(B * S, B_WIDTH),
            yc.reshape(B * S, C_V_WIDTH), row(attn_norm_g)[l], w_gate_b, row(b_gate)[l],
            proj_a_b, proj_b_b, proj_c_b, w_out_b)
        x = _ffn(l, x2.reshape(B, S, D), row(ffn_norm_g)[l], w_up_b, conv_w[l], row(conv_b)[l],
                 w_down_b, final_norm_g.reshape(1, D), final_norm=(l == L - 1))
    return x
```

```python
import functools
import math

import numpy as np
import jax
import jax.numpy as jnp
from jax import lax
from jax.experimental import pallas as pl
from jax.experimental.pallas import tpu as pltpu

D_MODEL = 1024
DEPTH = 4
A_WIDTH = 256
A_GROUPS = 4
A_GDIM = A_WIDTH // A_GROUPS
CHUNK = 128
B_WIDTH = 256
POOL_WINDOWS = (2, 4, 8, 16)
B_GDIM = B_WIDTH // len(POOL_WINDOWS)
C_HEADS = 4
C_HEAD_DIM = 64
C_VDIM = 2 * C_HEAD_DIM
C_QK_WIDTH = C_HEADS * 2 * C_HEAD_DIM
C_V_WIDTH = C_HEADS * C_VDIM
N_BUCKETS = 32
MAX_EXACT = 16
MAX_DISTANCE = 128
D_FF = 2816
EPS = 1e-6
SUBLN_EPS = 1e-5
LN_EPS = 1e-5
MASK_VALUE = -1e30
LOG2_E = math.log2(math.e)

O_A = 2 * A_WIDTH
O_B = O_A + B_WIDTH
O_Q = O_B + C_QK_WIDTH
O_K = O_Q + C_QK_WIDTH
IN_COLS = O_K + C_V_WIDTH

V7X_VMEM_LIMIT_BYTES = 56 * 1024 * 1024
HALO = 16
TM_IN = 512
TM_MERGE = 1024
TM_FFN = 1024
FF_CHUNK = 256
TQ = 1024
TK = 512
QB = 256
SCORE_LOOKAHEAD = 4

F32 = jnp.float32
BF16 = jnp.bfloat16


def _dot(a, b):
    return jnp.dot(a, b, preferred_element_type=F32)


def _gelu(x):
    return 0.5 * x * (1.0 + lax.erf(x * np.float32(math.sqrt(0.5))))


def _rmsnorm(x, g, eps):
    return x * lax.rsqrt(jnp.mean(x * x, axis=-1, keepdims=True) + eps) * g


def _const_spec(shape):
    nd = len(shape)
    return pl.BlockSpec(shape, lambda *_: (0,) * nd, pipeline_mode=pl.Buffered(1))


def _layer_spec(shape, layer):
    nd = len(shape)
    return pl.BlockSpec((None,) + tuple(shape), lambda *_: (layer,) + (0,) * nd,
                        pipeline_mode=pl.Buffered(1))


def _in_proj_kernel(x_ref, g_ref, w_ref, lng_ref, lnb_ref, sw_ref, sb_ref, pw_ref, ps_ref,
                    ya_ref, yb_ref, q_ref, k_ref, vt_ref, zb_scr):
    t = pl.program_id(1)
    tm = x_ref.shape[1]

    @pl.when(t == 0)
    def _():
        zb_scr[0:HALO, :] = jnp.zeros((HALO, B_WIDTH), F32)

    @pl.when(t > 0)
    def _():
        zb_scr[0:HALO, :] = zb_scr[tm:tm + HALO, :]

    h = _rmsnorm(x_ref[0], g_ref[...], EPS).astype(BF16)

    za = _dot(h, w_ref[:, 0:O_A])
    zb = _dot(h, w_ref[:, O_A:O_B])
    k_ref[0] = _dot(h, w_ref[:, O_Q:O_K]).astype(k_ref.dtype)

    za = _gelu(za)
    u = za[:, :A_WIDTH]
    v = za[:, A_WIDTH:]
    mu = jnp.mean(v, axis=-1, keepdims=True)
    vc = v - mu
    var = jnp.mean(vc * vc, axis=-1, keepdims=True)
    vn = (vc * lax.rsqrt(var + LN_EPS) * lng_ref[...] + lnb_ref[...]).astype(BF16)

    vt_ref[0, 0] = _dot(h, w_ref[:, O_K:IN_COLS]).T.astype(vt_ref.dtype)

    row = lax.broadcasted_iota(jnp.int32, (CHUNK, A_GROUPS * CHUNK), 0)
    col = lax.broadcasted_iota(jnp.int32, (CHUNK, A_GROUPS * CHUNK), 1)
    ws = jnp.where((col % CHUNK) <= row, sw_ref[...], 0.0).astype(BF16)
    grp = lax.broadcasted_iota(jnp.int32, (CHUNK, A_WIDTH), 1) // A_GDIM
    for c in range(tm // CHUNK):
        vch = vn[c * CHUNK:(c + 1) * CHUNK, :]
        rhs = jnp.concatenate(
            [jnp.where(grp == g, vch, jnp.zeros_like(vch)) for g in range(A_GROUPS)], axis=0)
        mixed = _dot(ws, rhs) + sb_ref[...]
        ya_ref[0, c * CHUNK:(c + 1) * CHUNK, :] = (
            u[c * CHUNK:(c + 1) * CHUNK, :] * mixed).astype(ya_ref.dtype)

    zq = _dot(h, w_ref[:, O_B:O_Q]) * np.float32(C_HEAD_DIM ** -0.5 * LOG2_E)
    first_half = (lax.broadcasted_iota(jnp.int32, (tm, C_QK_WIDTH), 1) % C_VDIM) < C_HEAD_DIM
    q_ref[0, 0] = jnp.where(first_half, zq, 0.0).astype(q_ref.dtype)
    q_ref[0, 1] = jnp.where(first_half, 0.0, zq).astype(q_ref.dtype)

    zb_scr[HALO:HALO + tm, :] = zb
    pos = t * tm + lax.broadcasted_iota(jnp.int32, (tm, 1), 0)
    lane = lax.broadcasted_iota(jnp.int32, (tm, 128), 1)
    halves = []
    for half in range(2):
        cols = slice(half * 128, (half + 1) * 128)
        w_small, w_big = POOL_WINDOWS[2 * half], POOL_WINDOWS[2 * half + 1]
        acc = zb[:, cols]
        for k in range(1, w_small):
            acc = acc + zb_scr[HALO - k:HALO - k + tm, cols]
        s_small = acc
        for k in range(w_small, w_big):
            acc = acc + zb_scr[HALO - k:HALO - k + tm, cols]
        cnt_small = jnp.minimum(pos + 1, w_small).astype(F32)
        cnt_big = jnp.minimum(pos + 1, w_big).astype(F32)
        pooled = jnp.where(lane < B_GDIM, s_small / cnt_small, acc / cnt_big)
        halves.append(pooled - zb[:, cols])
    p = jnp.concatenate(halves, axis=1).astype(BF16)
    yb_ref[0] = (_dot(p, pw_ref[...]) * ps_ref[...]).astype(yb_ref.dtype)


def _in_proj(layer, x, g, w_in, ln_g, ln_b, sgu_w_cat, sgu_b_tile, pool_w_bd, pool_scale):
    B, S, D = x.shape
    tm = TM_IN
    assert tm == TK, "the transposed value tiles are written one attention key tile at a time"
    grid = (B, S // tm)
    tok = lambda width: pl.BlockSpec((1, tm, width), lambda b, t: (b, t, 0))
    out_shape = (
        jax.ShapeDtypeStruct((B, S, A_WIDTH), BF16),
        jax.ShapeDtypeStruct((B, S, B_WIDTH), BF16),
        jax.ShapeDtypeStruct((B, 2, S, C_QK_WIDTH), BF16),
        jax.ShapeDtypeStruct((B, S, C_QK_WIDTH), BF16),
        jax.ShapeDtypeStruct((B, S // tm, C_V_WIDTH, tm), BF16),
    )
    return pl.pallas_call(
        _in_proj_kernel,
        grid=grid,
        in_specs=[
            tok(D),
            _const_spec((1, D)),
            _layer_spec((D, IN_COLS), layer),
            _const_spec((1, A_WIDTH)),
            _const_spec((1, A_WIDTH)),
            _const_spec((CHUNK, A_GROUPS * CHUNK)),
            _const_spec((CHUNK, A_WIDTH)),
            _const_spec((B_WIDTH, B_WIDTH)),
            _const_spec((1, B_WIDTH)),
        ],
        out_specs=(
            tok(A_WIDTH),
            tok(B_WIDTH),
            pl.BlockSpec((1, 2, tm, C_QK_WIDTH), lambda b, t: (b, 0, t, 0)),
            tok(C_QK_WIDTH),
            pl.BlockSpec((1, 1, C_V_WIDTH, tm), lambda b, t: (b, t, 0, 0)),
        ),
        out_shape=out_shape,
        scratch_shapes=[pltpu.VMEM((HALO + tm, B_WIDTH), F32)],
        compiler_params=pltpu.CompilerParams(
            dimension_semantics=("arbitrary", "arbitrary"),
            vmem_limit_bytes=V7X_VMEM_LIMIT_BYTES),
        name="in_proj",
    )(x, g, w_in, ln_g, ln_b, sgu_w_cat, sgu_b_tile, pool_w_bd, pool_scale)


N_BIAS_BLOCKS = -(-(TK + MAX_DISTANCE - 1) // QB)


def _bucket_tiles():
    r = np.arange(QB)[None, :]
    c = np.arange(TK)[:, None]
    tiles = []
    for dd in range(N_BIAS_BLOCKS):
        rel = dd * QB + r - c
        n = np.maximum(rel, 0)
        nf = np.maximum(n, 1).astype(np.float32)
        large = MAX_EXACT + (np.log(nf / MAX_EXACT) / math.log(MAX_DISTANCE / MAX_EXACT)
                             * (N_BUCKETS - MAX_EXACT)).astype(np.int32)
        large = np.minimum(large, N_BUCKETS - 1)
        bucket = np.where(n < MAX_EXACT, n, large)
        tiles.append(np.where(rel >= 0, bucket, -1))
    return np.stack(tiles).astype(np.int32)


def _bias_kernel(rb_ref, bk_ref, o_ref):
    for h in range(C_HEADS):
        far = rb_ref[N_BUCKETS - 1, h]
        for d in range(N_BIAS_BLOCKS):
            bk = bk_ref[d]
            acc = jnp.zeros(bk.shape, F32)
            for b in range(N_BUCKETS - 1):
                acc = jnp.where(bk == b, (rb_ref[b, h] - far) * np.float32(LOG2_E), acc)
            o_ref[h, d] = jnp.where(bk < 0, np.float32(MASK_VALUE), acc)


def _bias_tiles(rel_bias):
    buckets = jnp.asarray(_bucket_tiles())
    return pl.pallas_call(
        _bias_kernel,
        in_specs=[pl.BlockSpec(memory_space=pltpu.SMEM),
                  pl.BlockSpec(memory_space=pltpu.VMEM)],
        out_specs=pl.BlockSpec(memory_space=pltpu.VMEM),
        out_shape=jax.ShapeDtypeStruct((C_HEADS, N_BIAS_BLOCKS, TK, QB), F32),
        name="bias_tiles",
    )(rel_bias, buckets)


def _visible_keys(delta):
    return min(TK, max(0, delta + QB))


def _first_biased_key(delta):
    return min(TK, max(0, (delta - MAX_DISTANCE + 1) // 8 * 8))


def _attn_kernel(sc_ref, q_ref, k_ref, vt_ref, bias_ref, lamv_ref, g_ref, o_ref,
                 m_scr, l_scr, acc_scr):
    qi = pl.program_id(1)
    lam_init = sc_ref[0]
    lv = lamv_ref[...]
    lam = (jnp.exp(jnp.sum(lv[0:1] * lv[1:2], axis=-1, keepdims=True))
           - jnp.exp(jnp.sum(lv[2:3] * lv[3:4], axis=-1, keepdims=True)) + lam_init)

    m_scr[...] = jnp.full(m_scr.shape, -jnp.inf, F32)
    l_scr[...] = jnp.zeros(l_scr.shape, F32)
    acc_scr[...] = jnp.zeros(acc_scr.shape, F32)

    def step(j, koff, near=None):
        start = pl.multiple_of(j * TK, TK)
        chains = []
        for h in range(C_HEADS):
            for c in range(2):
                for b in range(TQ // QB):
                    delta = None if koff is None else b * QB - koff
                    nk = TK if koff is None else _visible_keys(delta)
                    if nk > 0:
                        chains.append((h, c, b, delta, nk))
        def score(chain):
            h, c, b, delta, nk = chain
            cols = slice(h * C_VDIM, (h + 1) * C_VDIM)
            kt = k_ref[0, pl.ds(start, nk), cols]
            qb = q_ref[0, c, b * QB:(b + 1) * QB, cols]
            s = lax.dot_general(
                kt, qb, (((1,), (1,)), ((), ())), preferred_element_type=F32)
            if koff is None:
                near_delta = b * QB + TK
                lo = _first_biased_key(near_delta)
                if lo < nk:
                    biased = s[lo:] + near * bias_ref[h, near_delta // QB, lo:nk, :]
                    s = jnp.concatenate([s[:lo], biased], axis=0)
                return s
            lo = _first_biased_key(delta)
            if lo < nk:
                biased = s[lo:] + bias_ref[h, delta // QB, lo:nk, :]
                s = biased if lo == 0 else jnp.concatenate([s[:lo], biased], axis=0)
            return s

        def consume(chain, s):
            h, c, b, delta, nk = chain
            cols = slice(h * C_VDIM, (h + 1) * C_VDIM)
            qsl = slice(c * TQ + b * QB, c * TQ + (b + 1) * QB)
            vt = vt_ref[0, j, cols, 0:nk]
            m_prev = m_scr[h, :, qsl]
            m_new = jnp.maximum(m_prev, jnp.max(s, axis=0, keepdims=True))
            alpha = jnp.exp2(m_prev - m_new)
            p = jnp.exp2(s - m_new)
            l_scr[h, :, qsl] = alpha * l_scr[h, :, qsl] + jnp.sum(p, axis=0, keepdims=True)
            acc_scr[h, :, qsl] = alpha * acc_scr[h, :, qsl] + _dot(vt, p.astype(BF16))
            m_scr[h, :, qsl] = m_new

        scores = [score(ch) for ch in chains[:SCORE_LOOKAHEAD]]
        for i, ch in enumerate(chains):
            if i + SCORE_LOOKAHEAD < len(chains):
                scores.append(score(chains[i + SCORE_LOOKAHEAD]))
            consume(ch, scores[i])
            scores[i] = None

    tiles_per_q = TQ // TK
    first_near = tiles_per_q * qi - 1

    def loop_step(j, carry):
        step(j, None, near=(j == first_near).astype(F32))
        return carry

    lax.fori_loop(0, first_near + 1, loop_step, 0)

    for t in range(tiles_per_q):
        step(tiles_per_q * qi + t, t * TK)

    for h in range(C_HEADS):
        cols = slice(h * C_VDIM, (h + 1) * C_VDIM)
        o_all = acc_scr[h] / l_scr[h]
        o = o_all[:, :TQ] - lam * o_all[:, TQ:]
        o = o * lax.rsqrt(jnp.mean(o * o, axis=0, keepdims=True) + SUBLN_EPS)
        o = o * g_ref[...] * (1.0 - lam_init)
        o_ref[0, :, cols] = o.T.astype(o_ref.dtype)


def _attention(scalars, q, k, vt, bias, lam_vecs, subln_g):
    B, S, _ = k.shape
    grid = (B, S // TQ)
    return pl.pallas_call(
        _attn_kernel,
        grid=grid,
        in_specs=[
            pl.BlockSpec(memory_space=pltpu.SMEM),
            pl.BlockSpec((1, 2, TQ, C_QK_WIDTH), lambda b, i: (b, 0, i, 0)),
            pl.BlockSpec((1, S, C_QK_WIDTH), lambda b, i: (b, 0, 0)),
            pl.BlockSpec((1, S // TK, C_V_WIDTH, TK), lambda b, i: (b, 0, 0, 0)),
            _const_spec((C_HEADS, N_BIAS_BLOCKS, TK, QB)),
            _const_spec((4, C_HEAD_DIM)),
            _const_spec((C_VDIM, 1)),
        ],
        out_specs=pl.BlockSpec((1, TQ, C_V_WIDTH), lambda b, i: (b, i, 0)),
        out_shape=jax.ShapeDtypeStruct((B, S, C_V_WIDTH), BF16),
        scratch_shapes=[
            pltpu.VMEM((C_HEADS, 1, 2 * TQ), F32),
            pltpu.VMEM((C_HEADS, 1, 2 * TQ), F32),
            pltpu.VMEM((C_HEADS, C_VDIM, 2 * TQ), F32),
        ],
        compiler_params=pltpu.CompilerParams(
            dimension_semantics=("arbitrary", "arbitrary"),
            vmem_limit_bytes=V7X_VMEM_LIMIT_BYTES),
        name="diff_attn",
    )(scalars, q, k, vt, bias, lam_vecs, subln_g)


def _merge_kernel(x_ref, ya_ref, yb_ref, yc_ref, g_ref, wg_ref, bg_ref,
                  pa_ref, pb_ref, pc_ref, wo_ref, o_ref):
    x = x_ref[...]
    h = _rmsnorm(x, g_ref[...], EPS).astype(BF16)
    merged = None
    for i, (y_ref, p_ref) in enumerate(((ya_ref, pa_ref), (yb_ref, pb_ref), (yc_ref, pc_ref))):
        cols = slice(i * D_MODEL, (i + 1) * D_MODEL)
        pre = _dot(h, wg_ref[:, cols]) + bg_ref[:, cols]
        gate = 1.0 / (1.0 + jnp.exp(-pre))
        term = gate * _dot(y_ref[...], p_ref[...])
        merged = term if merged is None else merged + term
    o_ref[...] = x + _dot(merged.astype(BF16), wo_ref[...])


def _merge(layer, x2, ya, yb, yc, g, w_gate, b_gate, proj_a, proj_b, proj_c, w_out):
    N, D = x2.shape
    tm = TM_MERGE
    tok = lambda width: pl.BlockSpec((tm, width), lambda t: (t, 0))
    return pl.pallas_call(
        _merge_kernel,
        grid=(N // tm,),
        in_specs=[
            tok(D), tok(A_WIDTH), tok(B_WIDTH), tok(C_V_WIDTH),
            _const_spec((1, D)),
            _layer_spec((D, 3 * D), layer),
            _const_spec((1, 3 * D)),
            _layer_spec((A_WIDTH, D), layer),
            _layer_spec((B_WIDTH, D), layer),
            _layer_spec((C_V_WIDTH, D), layer),
            _layer_spec((D, D), layer),
        ],
        out_specs=tok(D),
        out_shape=jax.ShapeDtypeStruct((N, D), F32),
        compiler_params=pltpu.CompilerParams(
            dimension_semantics=("arbitrary",),
            vmem_limit_bytes=V7X_VMEM_LIMIT_BYTES),
        name="merge",
    )(x2, ya, yb, yc, g, w_gate, b_gate, proj_a, proj_b, proj_c, w_out)


def _ffn_kernel(x_ref, g_ref, wu_ref, cw_ref, cb_ref, wd_ref, fg_ref, o_ref, halo_scr, act_scr,
                *, final_norm):
    t = pl.program_id(1)
    tm = x_ref.shape[1]
    x = x_ref[0]
    h = _rmsnorm(x, g_ref[...], EPS).astype(BF16)
    row = lax.broadcasted_iota(jnp.int32, (tm, FF_CHUNK), 0)

    @pl.when(t == 0)
    def _():
        halo_scr[...] = jnp.zeros(halo_scr.shape, F32)

    for c in range(D_FF // FF_CHUNK):
        cols = slice(c * FF_CHUNK, (c + 1) * FF_CHUNK)
        gate = _dot(h, wu_ref[:, cols])
        up = _dot(h, wu_ref[:, D_FF + c * FF_CHUNK:D_FF + (c + 1) * FF_CHUNK])
        prev1 = halo_scr[7:8, cols]
        prev2 = halo_scr[6:7, cols]
        g1 = jnp.where(row == 0, prev1, pltpu.roll(gate, 1, 0))
        g2 = jnp.where(row == 0, prev2, jnp.where(row == 1, prev1, pltpu.roll(gate, 2, 0)))
        halo_scr[:, cols] = gate[tm - 8:tm, :]
        conv = (g2 * cw_ref[0:1, cols] + g1 * cw_ref[1:2, cols] + gate * cw_ref[2:3, cols]
                + cb_ref[:, cols])
        act_scr[:, cols] = (_gelu(conv) * up).astype(BF16)
    out = x + _dot(act_scr[...], wd_ref[...])
    if final_norm:
        out = _rmsnorm(out, fg_ref[...], EPS)
    o_ref[0] = out


def _ffn(layer, x, g, w_up, conv_w, conv_b, w_down, final_g, final_norm):
    B, S, D = x.shape
    tm = TM_FFN
    tok = pl.BlockSpec((1, tm, D), lambda b, t: (b, t, 0))
    return pl.pallas_call(
        functools.partial(_ffn_kernel, final_norm=final_norm),
        grid=(B, S // tm),
        in_specs=[
            tok,
            _const_spec((1, D)),
            _layer_spec((D, 2 * D_FF), layer),
            _const_spec((3, D_FF)),
            _const_spec((1, D_FF)),
            _layer_spec((D_FF, D), layer),
            _const_spec((1, D)),
        ],
        out_specs=tok,
        out_shape=jax.ShapeDtypeStruct((B, S, D), F32),
        scratch_shapes=[pltpu.VMEM((8, D_FF), F32), pltpu.VMEM((tm, D_FF), BF16)],
        compiler_params=pltpu.CompilerParams(
            dimension_semantics=("arbitrary", "arbitrary"),
            vmem_limit_bytes=V7X_VMEM_LIMIT_BYTES),
        name="ffn_final" if final_norm else "ffn",
    )(x, g, w_up, conv_w, conv_b, w_down, final_g)


def kernel(x, attn_norm_g, w_in, w_gate, b_gate, sgu_ln_g, sgu_ln_b, sgu_w, sgu_b, proj_a, pool_w,
           pool_scale, proj_b, diff_lam, diff_subln_g, proj_c, w_out, ffn_norm_g, w_up, conv_w,
           conv_b, w_down, rel_bias, final_norm_g):
    B, S, D = x.shape
    L = DEPTH
    row = lambda a: a.reshape(a.shape[0], 1, a.shape[1])

    w_in_b = w_in.astype(BF16)
    w_gate_b = w_gate.astype(BF16)
    proj_a_b = proj_a.astype(BF16)
    proj_b_b = proj_b.astype(BF16)
    proj_c_b = proj_c.astype(BF16)
    w_out_b = w_out.astype(BF16)
    w_up_b = w_up.astype(BF16)
    w_down_b = w_down.astype(BF16)
    sgu_w_cat = jnp.transpose(sgu_w, (0, 2, 1, 3)).reshape(L, CHUNK, A_GROUPS * CHUNK)
    sgu_b_tile = jnp.repeat(jnp.transpose(sgu_b, (0, 2, 1)), A_GDIM, axis=2)
    eye = jnp.eye(len(POOL_WINDOWS), dtype=pool_w.dtype)
    pool_w_bd = jnp.einsum('lgcd,gh->lgchd', pool_w, eye).reshape(L, B_WIDTH, B_WIDTH).astype(BF16)

    bias = _bias_tiles(rel_bias)

    for l in range(L):
        lam_init = 0.8 - 0.6 * math.exp(-0.3 * l)
        scalars = jnp.asarray([lam_init], F32)
        ya, yb, q, k, vt = _in_proj(
            l, x, row(attn_norm_g)[l], w_in_b, row(sgu_ln_g)[l], row(sgu_ln_b)[l],
            sgu_w_cat[l], sgu_b_tile[l], pool_w_bd[l], row(pool_scale)[l])
        yc = _attention(scalars, q, k, vt, bias, diff_lam[l], diff_subln_g[l].reshape(C_VDIM, 1))
        x2 = _merge(
            l, x.reshape(B * S, D), ya.reshape(B * S, A_WIDTH), yb.reshape(B * S, B_WIDTH),
            yc.reshape(B * S, C_V_WIDTH), row(attn_norm_g)[l], w_gate_b, row(b_gate)[l],
            proj_a_b, proj_b_b, proj_c_b, w_out_b)
        x = _ffn(l, x2.reshape(B, S, D), row(ffn_norm_g)[l], w_up_b, conv_w[l], row(conv_b)[l],
                 w_down_b, final_norm_g.reshape(1, D), final_norm=(l == L - 1))
    return x
```
